```python
import math
import jax, jax.numpy as jnp
from jax import lax
import numpy as np

D_MODEL = 1024
BATCH = 16
SEQ = 2048
DEPTH = 2

GRID_W = 64
CTX_LEN = 256
EPS = 1e-6

NA_HEADS = 4
NA_DIM = 64
WIN_R = 8
WIN_C = 16
DIFF_HEADS = 4
DIFF_QK_DIM = 32
DIFF_V_DIM = 64
MLA_HEADS = 8
MLA_NOPE = 64
MLA_ROPE = 32
MLA_V = 64
MLA_Q_RANK = 256
MLA_KV_RANK = 128
ROPE_DIM = 32
ROPE_THETA = 10000.0
Q_BLOCK = 128

NA_W = NA_HEADS * NA_DIM
DIFF_W = DIFF_HEADS * DIFF_V_DIM
MLA_W = MLA_HEADS * MLA_V
MIX_W = NA_W + DIFF_W + MLA_W
NA_COLS = 3 * NA_W
DIFF_QK_COLS = DIFF_HEADS * 2 * DIFF_QK_DIM
DIFF_COLS = 2 * DIFF_QK_COLS + DIFF_W
MLA_COLS = MLA_Q_RANK + MLA_KV_RANK + MLA_ROPE
IN_COLS = NA_COLS + DIFF_COLS + MLA_COLS

N_EXPERTS = 64
TOP_K = 8
N_GROUPS = 8
TOPK_GROUPS = 4
EXPERT_FF = 256
SHARED_FF = 256
ROUTED_SCALE = 2.5
MOE_BLOCK = 128

kernel_name = 'hybrid_diffusion_block'


def _rmsnorm(x, g):
    xf = x.astype(jnp.float32)
    y = xf * lax.rsqrt(jnp.mean(xf * xf, axis=-1, keepdims=True) + EPS)
    return (y * g.astype(jnp.float32)).astype(x.dtype)


def _softmax(s):
    return jax.nn.softmax(s.astype(jnp.float32), axis=-1)


def _rope_tables(n_tokens, dtype):
    t = jnp.arange(n_tokens)
    row = (t // GRID_W).astype(jnp.float32)
    col = (t % GRID_W).astype(jnp.float32)
    n_freq = ROPE_DIM // 4
    freq = ROPE_THETA ** (-jnp.arange(n_freq, dtype=jnp.float32) / n_freq)
    ang = jnp.concatenate([row[:, None] * freq, col[:, None] * freq], axis=-1)
    return jnp.cos(ang).astype(dtype), jnp.sin(ang).astype(dtype)


def _apply_rope(x, cos, sin):
    shape = (cos.shape[0],) + (1,) * (x.ndim - 3) + (cos.shape[-1],)
    cos = cos.reshape(shape)
    sin = sin.reshape(shape)
    half = x.shape[-1] // 2
    x1, x2 = x[..., :half], x[..., half:]
    return jnp.concatenate([x1 * cos - x2 * sin, x1 * sin + x2 * cos], axis=-1)


def _sweep_query_blocks(fn, q):
    b, s = q.shape[:2]
    nb = s // Q_BLOCK
    qb = jnp.moveaxis(q.reshape((b, nb, Q_BLOCK) + q.shape[2:]), 1, 0)
    out = jnp.moveaxis(lax.map(fn, qb), 0, 1)
    return out.reshape((b, s) + out.shape[3:])


def _attend(q, k, v, scale):
    s = jnp.einsum('bqhd,bkhd->bhqk', q, k).astype(jnp.float32) * scale
    p = _softmax(s).astype(v.dtype)
    return jnp.einsum('bhqk,bkhd->bqhd', p, v)


def _diff_attend(q, k, v, lam):
    s = jnp.einsum('bqhmd,bkhmd->bhmqk', q, k).astype(jnp.float32) * (q.shape[-1] ** -0.5)
    p = _softmax(s)
    a = (p[:, :, 0] - lam * p[:, :, 1]).astype(v.dtype)
    return jnp.einsum('bhqk,bkhd->bqhd', a, v)


def _na_latent(q, k, v, k_ctx, v_ctx, rel_bias):
    b, s, h, d = q.shape
    rows = s // GRID_W
    wr = min(WIN_R, rows)
    scale = d ** -0.5
    qg = jnp.moveaxis(q.reshape(b, rows, GRID_W, h, d), 1, 0)
    kg = k.reshape(b, rows, GRID_W, h, d)
    vg = v.reshape(b, rows, GRID_W, h, d)
    cols = jnp.arange(GRID_W)
    col_start = jnp.clip(cols - WIN_C // 2, 0, GRID_W - WIN_C)
    col_idx = col_start[:, None] + jnp.arange(WIN_C)[None, :]
    dc = col_idx - cols[:, None] + (WIN_C - 1)

    def row_fn(args):
        r, q_row = args
        rs = jnp.clip(r - wr // 2, 0, rows - wr)
        k_nb = jnp.take(lax.dynamic_slice_in_dim(kg, rs, wr, axis=1), col_idx, axis=2)
        v_nb = jnp.take(lax.dynamic_slice_in_dim(vg, rs, wr, axis=1), col_idx, axis=2)
        dr = rs + jnp.arange(wr) - r + (WIN_R - 1)
        bias = rel_bias[:, dr[None, :, None], dc[:, None, :]]
        s_nb = jnp.einsum('bqhd,brqchd->bhqrc', q_row, k_nb).astype(jnp.float32) * scale + bias
        s_nb = s_nb.reshape(b, h, GRID_W, wr * WIN_C)
        s_ctx = jnp.einsum('bqhd,bchd->bhqc', q_row, k_ctx).astype(jnp.float32) * scale
        p = _softmax(jnp.concatenate([s_nb, s_ctx], axis=-1)).astype(v.dtype)
        p_nb = p[..., :wr * WIN_C].reshape(b, h, GRID_W, wr, WIN_C)
        p_ctx = p[..., wr * WIN_C:]
        return (jnp.einsum('bhqrc,brqchd->bqhd', p_nb, v_nb)
                + jnp.einsum('bhqc,bchd->bqhd', p_ctx, v_ctx))

    out = lax.map(row_fn, (jnp.arange(rows), qg))
    return jnp.moveaxis(out, 0, 1).reshape(b, s, h * d)


def _mixer(h_lat, h_ctx, cos, sin, layer_idx, need_ctx, w_in, rel_bias, lam_vec, subln_g,
           q_norm_g, w_uq, kv_norm_g, w_ukv, w_out):
    o_diff = NA_COLS
    o_mla = NA_COLS + DIFF_COLS
    p_lat = h_lat @ w_in
    p_ctx = h_ctx @ w_in

    def na_q(p):
        return p[..., :NA_W].reshape(p.shape[:2] + (NA_HEADS, NA_DIM))

    def na_kv(p):
        sh = p.shape[:2] + (NA_HEADS, NA_DIM)
        return p[..., NA_W:2 * NA_W].reshape(sh), p[..., 2 * NA_W:NA_COLS].reshape(sh)

    def diff_q(p):
        return p[..., o_diff:o_diff + DIFF_QK_COLS].reshape(p.shape[:2] + (DIFF_HEADS, 2, DIFF_QK_DIM))

    def diff_kv(p):
        k = p[..., o_diff + DIFF_QK_COLS:o_diff + 2 * DIFF_QK_COLS].reshape(
            p.shape[:2] + (DIFF_HEADS, 2, DIFF_QK_DIM))
        v = p[..., o_diff + 2 * DIFF_QK_COLS:o_mla].reshape(p.shape[:2] + (DIFF_HEADS, DIFF_V_DIM))
        return k, v

    def mla_q(p):
        cq = _rmsnorm(p[..., o_mla:o_mla + MLA_Q_RANK], q_norm_g)
        return (cq @ w_uq).reshape(p.shape[:2] + (MLA_HEADS, MLA_NOPE + MLA_ROPE))

    def mla_kv(p, rotate):
        ckv = _rmsnorm(p[..., o_mla + MLA_Q_RANK:o_mla + MLA_Q_RANK + MLA_KV_RANK], kv_norm_g)
        kv = (ckv @ w_ukv).reshape(p.shape[:2] + (MLA_HEADS, MLA_NOPE + MLA_V))
        k_rope = p[..., o_mla + MLA_Q_RANK + MLA_KV_RANK:IN_COLS]
        if rotate:
            k_rope = _apply_rope(k_rope, cos, sin)
        k_rope = jnp.broadcast_to(k_rope[:, :, None, :], p.shape[:2] + (MLA_HEADS, MLA_ROPE))
        return jnp.concatenate([kv[..., :MLA_NOPE], k_rope], axis=-1), kv[..., MLA_NOPE:]

    def rope_tail(q):
        return jnp.concatenate([q[..., :MLA_NOPE], _apply_rope(q[..., MLA_NOPE:], cos, sin)], axis=-1)

    lambda_init = 0.8 - 0.6 * math.exp(-0.3 * layer_idx)
    lv = lam_vec.astype(jnp.float32)
    lam = jnp.exp(jnp.sum(lv[0] * lv[1])) - jnp.exp(jnp.sum(lv[2] * lv[3])) + lambda_init

    def diff_post(o):
        o = _rmsnorm(o, subln_g) * (1.0 - lambda_init)
        return o.reshape(o.shape[:2] + (DIFF_W,))

    mla_scale = (MLA_NOPE + MLA_ROPE) ** -0.5

    k_na_c, v_na_c = na_kv(p_ctx)
    k_d_c, v_d_c = diff_kv(p_ctx)
    k_m_c, v_m_c = mla_kv(p_ctx, False)

    k_na, v_na = na_kv(p_lat)
    na_out = _na_latent(na_q(p_lat), k_na, v_na, k_na_c, v_na_c, rel_bias)

    k_d, v_d = diff_kv(p_lat)
    k_d_all = jnp.concatenate([_apply_rope(k_d, cos, sin), k_d_c], axis=1)
    v_d_all = jnp.concatenate([v_d, v_d_c], axis=1)
    d_out = _sweep_query_blocks(lambda qb: _diff_attend(qb, k_d_all, v_d_all, lam),
                                _apply_rope(diff_q(p_lat), cos, sin))

    k_m, v_m = mla_kv(p_lat, True)
    k_m_all = jnp.concatenate([k_m, k_m_c], axis=1)
    v_m_all = jnp.concatenate([v_m, v_m_c], axis=1)
    m_out = _sweep_query_blocks(lambda qb: _attend(qb, k_m_all, v_m_all, mla_scale), rope_tail(mla_q(p_lat)))
    b, s = h_lat.shape[:2]
    out_lat = jnp.concatenate([na_out, diff_post(d_out), m_out.reshape(b, s, MLA_W)], axis=-1) @ w_out

    if not need_ctx:
        return out_lat, None
    bc, c = h_ctx.shape[:2]
    na_c = _attend(na_q(p_ctx), k_na_c, v_na_c, NA_DIM ** -0.5).reshape(bc, c, NA_W)
    d_c = diff_post(_diff_attend(diff_q(p_ctx), k_d_c, v_d_c, lam))
    m_c = _attend(mla_q(p_ctx), k_m_c, v_m_c, mla_scale).reshape(bc, c, MLA_W)
    out_ctx = jnp.concatenate([na_c, d_c, m_c], axis=-1) @ w_out
    return out_lat, out_ctx


def _moe(h, router_w, router_bias, wg, wu, wd, sg, su, sd):
    b, l, d = h.shape
    hb_all = h.reshape(b * l // MOE_BLOCK, MOE_BLOCK, d)

    def block_fn(hb):
        scores = jax.nn.sigmoid((hb @ router_w).astype(jnp.float32))
        sel = scores + router_bias.astype(jnp.float32)
        grp = lax.top_k(sel.reshape(MOE_BLOCK, N_GROUPS, N_EXPERTS // N_GROUPS), 2)[0].sum(-1)
        _, gidx = lax.top_k(grp, TOPK_GROUPS)
        gmask = jnp.sum(jax.nn.one_hot(gidx, N_GROUPS, dtype=jnp.float32), axis=1) > 0
        emask = jnp.repeat(gmask, N_EXPERTS // N_GROUPS, axis=-1)
        _, eidx = lax.top_k(jnp.where(emask, sel, -jnp.inf), TOP_K)
        w = jnp.take_along_axis(scores, eidx, axis=-1)
        w = w / jnp.sum(w, axis=-1, keepdims=True) * ROUTED_SCALE
        gates = jnp.sum(jax.nn.one_hot(eidx, N_EXPERTS, dtype=jnp.float32) * w[..., None], axis=1)
        act = jax.nn.silu(jnp.einsum('qd,edf->qef', hb, wg)) * jnp.einsum('qd,edf->qef', hb, wu)
        routed = jnp.einsum('qef,efd->qd', act * gates[..., None].astype(act.dtype), wd)
        shared = (jax.nn.silu(hb @ sg) * (hb @ su)) @ sd
        return routed + shared

    return lax.map(block_fn, hb_all).reshape(b, l, d)


def setup_inputs(seed: int = 0) -> dict:
    key = jax.random.key(seed)
    ks = jax.random.split(key, 32)
    f32 = jnp.float32

    def nrm(k, shape, scale):
        return jax.random.normal(k, shape, f32) * scale

    L, D, E = DEPTH, D_MODEL, N_EXPERTS
    return {
        'x': nrm(ks[0], (BATCH, SEQ, D), 1.0),
        'c': nrm(ks[1], (BATCH, D), 1.0),
        'ctx': nrm(ks[2], (BATCH, CTX_LEN, D), 1.0),
        'c_ctx': nrm(ks[3], (D,), 1.0),
        'w_ada': nrm(ks[4], (L, D, 6 * D), 0.5 * D ** -0.5),
        'b_ada': nrm(ks[5], (L, 6 * D), 0.02),
        'norm_attn_g': 1.0 + nrm(ks[6], (L, D), 0.02),
        'norm_ffn_g': 1.0 + nrm(ks[7], (L, D), 0.02),
        'w_in': nrm(ks[8], (L, D, IN_COLS), D ** -0.5),
        'na_rel_bias': nrm(ks[9], (L, NA_HEADS, 2 * WIN_R - 1, 2 * WIN_C - 1), 0.1),
        'diff_lambda': nrm(ks[10], (L, 4, DIFF_QK_DIM), 0.1),
        'diff_subln_g': 1.0 + nrm(ks[11], (L, DIFF_V_DIM), 0.02),
        'mla_q_norm_g': 1.0 + nrm(ks[12], (L, MLA_Q_RANK), 0.02),
        'mla_w_uq': nrm(ks[13], (L, MLA_Q_RANK, MLA_HEADS * (MLA_NOPE + MLA_ROPE)), MLA_Q_RANK ** -0.5),
        'mla_kv_norm_g': 1.0 + nrm(ks[14], (L, MLA_KV_RANK), 0.02),
        'mla_w_ukv': nrm(ks[15], (L, MLA_KV_RANK, MLA_HEADS * (MLA_NOPE + MLA_V)), MLA_KV_RANK ** -0.5),
        'w_out': nrm(ks[16], (L, MIX_W, D), MIX_W ** -0.5),
        'router_w': nrm(ks[17], (L, D, E), D ** -0.5),
        'router_bias': nrm(ks[18], (L, E), 0.01),
        'expert_w_gate': nrm(ks[19], (L, E, D, EXPERT_FF), D ** -0.5),
        'expert_w_up': nrm(ks[20], (L, E, D, EXPERT_FF), D ** -0.5),
        'expert_w_down': nrm(ks[21], (L, E, EXPERT_FF, D), EXPERT_FF ** -0.5),
        'shared_w_gate': nrm(ks[22], (L, D, SHARED_FF), D ** -0.5),
        'shared_w_up': nrm(ks[23], (L, D, SHARED_FF), D ** -0.5),
        'shared_w_down': nrm(ks[24], (L, SHARED_FF, D), SHARED_FF ** -0.5),
        'final_norm_g': 1.0 + nrm(ks[25], (D,), 0.02),
    }


def reference(x, c, ctx, c_ctx, w_ada, b_ada, norm_attn_g, norm_ffn_g, w_in, na_rel_bias, diff_lambda,
              diff_subln_g, mla_q_norm_g, mla_w_uq, mla_kv_norm_g, mla_w_ukv, w_out, router_w, router_bias,
              expert_w_gate, expert_w_up, expert_w_down, shared_w_gate, shared_w_up, shared_w_down,
              final_norm_g):
    cos, sin = _rope_tables(x.shape[1], x.dtype)
    xc = ctx
    for l in range(DEPTH):
        need_ctx = l < DEPTH - 1
        mod = (jax.nn.silu(c) @ w_ada[l] + b_ada[l])[:, None, :]
        mod_c = jax.nn.silu(c_ctx) @ w_ada[l] + b_ada[l]
        sh1, sc1, g1, sh2, sc2, g2 = jnp.split(mod, 6, axis=-1)
        sh1c, sc1c, g1c, sh2c, sc2c, g2c = jnp.split(mod_c, 6, axis=-1)
        h = _rmsnorm(x, norm_attn_g[l]) * (1.0 + sc1) + sh1
        hc = _rmsnorm(xc, norm_attn_g[l]) * (1.0 + sc1c) + sh1c
        a_lat, a_ctx = _mixer(h, hc, cos, sin, l, need_ctx, w_in[l], na_rel_bias[l], diff_lambda[l],
                              diff_subln_g[l], mla_q_norm_g[l], mla_w_uq[l], mla_kv_norm_g[l], mla_w_ukv[l],
                              w_out[l])
        x = x + g1 * a_lat
        h2 = _rmsnorm(x, norm_ffn_g[l]) * (1.0 + sc2) + sh2
        x = x + g2 * _moe(h2, router_w[l], router_bias[l], expert_w_gate[l], expert_w_up[l],
                          expert_w_down[l], shared_w_gate[l], shared_w_up[l], shared_w_down[l])
        if need_ctx:
            xc = xc + g1c * a_ctx
            h2c = _rmsnorm(xc, norm_ffn_g[l]) * (1.0 + sc2c) + sh2c
            xc = xc + g2c * _moe(h2c, router_w[l], router_bias[l], expert_w_gate[l], expert_w_up[l],
                                 expert_w_down[l], shared_w_gate[l], shared_w_up[l], shared_w_down[l])
    return _rmsnorm(x, final_norm_g)
```

```python
import functools
import math

import numpy as np
import jax
import jax.numpy as jnp
from jax import lax
from jax.experimental import pallas as pl
from jax.experimental.pallas import tpu as pltpu

F32 = jnp.float32
BF16 = jnp.bfloat16

D_MODEL = 1024
DEPTH = 2
GRID_W = 64
EPS = 1e-6
NA_HEADS, NA_DIM, WIN_R, WIN_C = 4, 64, 8, 16
DIFF_HEADS, DIFF_QK_DIM, DIFF_V_DIM = 4, 32, 64
MLA_HEADS, MLA_NOPE, MLA_ROPE, MLA_V = 8, 64, 32, 64
MLA_Q_RANK, MLA_KV_RANK = 256, 128
ROPE_DIM = 32
ROPE_THETA = 10000.0
N_EXPERTS, TOP_K, N_GROUPS, TOPK_GROUPS = 64, 8, 8, 4
EXPERT_FF = 256
ROUTED_SCALE = 2.5

NA_W = NA_HEADS * NA_DIM
DIFF_W = DIFF_HEADS * DIFF_V_DIM
MLA_W = MLA_HEADS * MLA_V
O_DIFF = 3 * NA_W
O_MLA = O_DIFF + 3 * DIFF_W
IN_COLS = O_MLA + MLA_Q_RANK + MLA_KV_RANK + MLA_ROPE
HALF = ROPE_DIM // 2

LANES = 128
TQ = 256
Q_ROWS = TQ // GRID_W
NA_WIN_ROWS = Q_ROWS + WIN_R
NA_WIN = NA_WIN_ROWS * GRID_W
SLOT = 2 * LANES
W_IN_COLS = 3 * NA_W + 3 * DIFF_W + MLA_Q_RANK + 3 * LANES
EXPERTS_PER_STEP = 4
NEG = -1e30
VMEM_LIMIT = 56 * 1024 * 1024


def _dot(a, b):
    return jnp.dot(a, b, preferred_element_type=F32)


def _dot_nt(a, b):
    return lax.dot_general(a, b, (((1,), (1,)), ((), ())), preferred_element_type=F32)


def _split(a):
    hi = a.astype(BF16)
    lo = (a - hi.astype(F32)).astype(BF16)
    return hi, lo


def _rms(x):
    return x * lax.rsqrt(jnp.mean(x * x, axis=-1, keepdims=True) + EPS)


def _silu(x):
    return x * jax.nn.sigmoid(x)


def _params(sem):
    return pltpu.CompilerParams(dimension_semantics=sem, vmem_limit_bytes=VMEM_LIMIT)


def _ada_kernel(c_ref, w_ref, b_ref, o_ref):
    a_hi, a_lo = _split(_silu(c_ref[...]))
    w_hi, w_lo = _split(w_ref[0])
    o_ref[0] = _dot(a_hi, w_hi) + _dot(a_hi, w_lo) + _dot(a_lo, w_hi) + b_ref[0]


def _ada_mod(c_all, w_ada, b_ada):
    depth, d, n = w_ada.shape
    rows = c_all.shape[0]
    nb = 1536
    return pl.pallas_call(
        _ada_kernel,
        grid=(depth, n // nb),
        in_specs=[pl.BlockSpec((rows, d), lambda l, j: (0, 0)),
                  pl.BlockSpec((1, d, nb), lambda l, j: (l, 0, j)),
                  pl.BlockSpec((1, 1, nb), lambda l, j: (l, 0, j))],
        out_specs=pl.BlockSpec((1, rows, nb), lambda l, j: (l, 0, j)),
        out_shape=jax.ShapeDtypeStruct((depth, rows, n), F32),
        compiler_params=_params(("arbitrary", "arbitrary")),
    )(c_all, w_ada, b_ada.reshape(depth, 1, n))


def _pre_kernel(x_ref, mod_ref, g_ref, win_ref, qg_ref, wuq_ref, kvg_ref, wq2_ref,
                cosd_ref, sind_ref, cosk_ref, sink_ref,
                qn_ref, kn_ref, vn_ref, qd_ref, kd_ref, vd_ref, qm_ref, kvm_ref):
    h = _rms(x_ref[0]) * g_ref[...]
    h = h * (1.0 + mod_ref[0, 1:2, :]) + mod_ref[0, 0:1, :]
    p = _dot(h.astype(BF16), win_ref[...])
    cd, sd = cosd_ref[...], sind_ref[...]

    qn_ref[0] = (p[:, 0:256] * (NA_DIM ** -0.5)).astype(BF16)
    kn_ref[0] = p[:, 256:512].astype(BF16)
    vn_ref[0] = p[:, 512:768].astype(BF16)

    qa, qb = p[:, 768:896], p[:, 896:1024]
    sc = DIFF_QK_DIM ** -0.5
    qd_ref[0, :, 0:LANES] = ((qa * cd - qb * sd) * sc).astype(BF16)
    qd_ref[0, :, LANES:2 * LANES] = ((qa * sd + qb * cd) * sc).astype(BF16)
    ka, kb = p[:, 1024:1152], p[:, 1152:1280]
    kd_ref[0, :, 0:LANES] = (ka * cd - kb * sd).astype(BF16)
    kd_ref[0, :, LANES:2 * LANES] = (ka * sd + kb * cd).astype(BF16)
    vd_ref[0] = p[:, 1280:1536].astype(BF16)

    cq = (_rms(p[:, 1536:1792]) * qg_ref[...]).astype(BF16)
    q1 = _dot(cq, wuq_ref[...])
    ms = (MLA_NOPE + MLA_ROPE) ** -0.5
    ra, rb = q1[:, 512:640], q1[:, 640:768]
    q2 = jnp.concatenate([(q1[:, 0:512] * ms).astype(BF16),
                          ((ra * cd - rb * sd) * ms).astype(BF16),
                          ((ra * sd + rb * cd) * ms).astype(BF16)], axis=-1)
    qm_ref[0] = _dot(q2, wq2_ref[...]).astype(BF16)

    kvm_ref[0, :, 0:LANES] = (_rms(p[:, 1792:1920]) * kvg_ref[...]).astype(BF16)
    kvm_ref[0, :, LANES:2 * LANES] = (p[:, 1920:2048] * cosk_ref[...] + p[:, 2048:2176] * sink_ref[...]).astype(BF16)


def _pre(x_all, mod, g, win, qg, wuq, kvg, wq2, tabs, s_lat):
    b, t, d = x_all.shape
    nj = t // TQ
    njl = s_lat // TQ
    tok = lambda w: pl.BlockSpec((1, TQ, w), lambda i, j: (i, j, 0))
    full = lambda a: pl.BlockSpec(a.shape, lambda i, j: (0,) * a.ndim)
    tab = pl.BlockSpec((TQ, LANES), lambda i, j: (j, 0))
    widths = [NA_W, NA_W, NA_W, 2 * LANES, 2 * LANES, DIFF_W, MLA_HEADS * SLOT, SLOT]
    return pl.pallas_call(
        _pre_kernel,
        grid=(b, nj),
        in_specs=[tok(d),
                  pl.BlockSpec((1, 6, d), lambda i, j: (jnp.where(j < njl, i, b), 0, 0)),
                  full(g), full(win), full(qg), full(wuq), full(kvg), full(wq2), tab, tab, tab, tab],
        out_specs=[tok(w) for w in widths],
        out_shape=[jax.ShapeDtypeStruct((b, t, w), BF16) for w in widths],
        compiler_params=_params(("arbitrary", "arbitrary")),
    )(x_all, mod, g, win, qg, wuq, kvg, wq2, *tabs)


def _lane_mask(shape, lo, width):
    lane = lax.broadcasted_iota(jnp.int32, shape, len(shape) - 1)
    return (lane >= lo) & (lane < lo + width)


def _na_kernel(q_ref, k_ref, v_ref, bias_ref, o_ref, *, s_lat, n_ctx):
    j = pl.program_id(1)
    njl = s_lat // TQ
    rows = s_lat // GRID_W
    q = q_ref[0]
    kc = k_ref[0, s_lat:s_lat + n_ctx, :]
    vc = v_ref[0, s_lat:s_lat + n_ctx, :]

    def heads(latent):
        if latent:
            w0 = jnp.clip(j * Q_ROWS - WIN_R // 2, 0, rows - NA_WIN_ROWS)
            start = pl.multiple_of(w0 * GRID_W, GRID_W)
            kw = k_ref[0, pl.ds(start, NA_WIN), :]
            vw = v_ref[0, pl.ds(start, NA_WIN), :]
        acc = jnp.zeros((TQ, NA_W), F32)
        for h in range(NA_HEADS):
            mh = _lane_mask((TQ, NA_W), h * NA_DIM, NA_DIM)
            qh = jnp.where(mh, q, jnp.zeros_like(q))
            s2 = _dot_nt(qh, kc)
            m = jnp.max(s2, axis=-1, keepdims=True)
            if latent:
                s1 = _dot_nt(qh, kw) + bias_ref[0, h]
                m = jnp.maximum(m, jnp.max(s1, axis=-1, keepdims=True))
                e1 = jnp.exp(s1 - m)
            e2 = jnp.exp(s2 - m)
            l = jnp.sum(e2, axis=-1, keepdims=True)
            o = _dot(e2.astype(BF16), vc)
            if latent:
                l = l + jnp.sum(e1, axis=-1, keepdims=True)
                o = o + _dot(e1.astype(BF16), vw)
            acc = acc + jnp.where(mh, o * (1.0 / l), 0.0)
        o_ref[0] = acc.astype(BF16)

    @pl.when(j < njl)
    def _():
        heads(True)

    @pl.when(j >= njl)
    def _():
        heads(False)


def _diff_kernel(q_ref, k_ref, v_ref, lam_ref, gsub_ref, o_ref, *, s_lat, n_ctx, lambda_init):
    j = pl.program_id(1)
    njl = s_lat // TQ
    q = q_ref[0]
    lv = lam_ref[...]
    lam = (jnp.exp(jnp.sum(lv[0:1] * lv[1:2], axis=-1, keepdims=True))
           - jnp.exp(jnp.sum(lv[2:3] * lv[3:4], axis=-1, keepdims=True)) + lambda_init)

    def heads(k, v):
        acc = jnp.zeros((TQ, DIFF_W), F32)
        for h in range(DIFF_HEADS):
            es, ls = [], []
            for m in range(2):
                lo = (2 * h + m) * HALF
                mm = _lane_mask((TQ, 2 * LANES), lo, HALF) | _lane_mask((TQ, 2 * LANES), LANES + lo, HALF)
                s = _dot_nt(jnp.where(mm, q, jnp.zeros_like(q)), k)
                e = jnp.exp(s - jnp.max(s, axis=-1, keepdims=True))
                es.append(e)
                ls.append(jnp.sum(e, axis=-1, keepdims=True))
            a = es[0] * (1.0 / ls[0]) - es[1] * (lam / ls[1])
            o = _dot(a.astype(BF16), v)
            om = jnp.where(_lane_mask((TQ, DIFF_W), h * DIFF_V_DIM, DIFF_V_DIM), o, 0.0)
            ms = jnp.sum(om * om, axis=-1, keepdims=True) * (1.0 / DIFF_V_DIM)
            acc = acc + om * lax.rsqrt(ms + EPS)
        o_ref[0] = (acc * gsub_ref[...]).astype(BF16)

    @pl.when(j < njl)
    def _():
        heads(k_ref[0], v_ref[0])

    @pl.when(j >= njl)
    def _():
        heads(k_ref[0, s_lat:s_lat + n_ctx, :], v_ref[0, s_lat:s_lat + n_ctx, :])


def _mla_kernel(q_ref, kv_ref, o_ref, *, s_lat, n_ctx):
    j = pl.program_id(1)
    njl = s_lat // TQ

    def heads(kv):
        v = kv[:, 0:LANES]
        for h in range(MLA_HEADS):
            s = _dot_nt(q_ref[0, :, h * SLOT:(h + 1) * SLOT], kv)
            e = jnp.exp(s - jnp.max(s, axis=-1, keepdims=True))
            l = jnp.sum(e, axis=-1, keepdims=True)
            o_ref[0, :, h * LANES:(h + 1) * LANES] = (_dot(e.astype(BF16), v) * (1.0 / l)).astype(BF16)

    @pl.when(j < njl)
    def _():
        heads(kv_ref[0])

    @pl.when(j >= njl)
    def _():
        heads(kv_ref[0, s_lat:s_lat + n_ctx, :])


def _attention(qn, kn, vn, bias, qd, kd, vd, lam, gsub, qm, kvm, s_lat, n_ctx, nj, lambda_init):
    b, t, _ = qn.shape
    njl = s_lat // TQ
    blk = lambda w: pl.BlockSpec((1, TQ, w), lambda i, j: (i, j, 0))
    per_b = lambda w: pl.BlockSpec((1, t, w), lambda i, j: (i, 0, 0))
    full = lambda a: pl.BlockSpec(a.shape, lambda i, j: (0,) * a.ndim)
    sem = _params(("arbitrary", "arbitrary"))
    variant = lambda i, j: (jnp.where(j == 0, 0, jnp.where(j >= njl - 1, 2, 1)), 0, 0, 0)
    na = pl.pallas_call(
        functools.partial(_na_kernel, s_lat=s_lat, n_ctx=n_ctx),
        grid=(b, nj),
        in_specs=[blk(NA_W), per_b(NA_W), per_b(NA_W),
                  pl.BlockSpec((1, NA_HEADS, TQ, NA_WIN), variant)],
        out_specs=blk(NA_W),
        out_shape=jax.ShapeDtypeStruct((b, t, NA_W), BF16),
        compiler_params=sem,
    )(qn, kn, vn, bias)
    df = pl.pallas_call(
        functools.partial(_diff_kernel, s_lat=s_lat, n_ctx=n_ctx, lambda_init=lambda_init),
        grid=(b, nj),
        in_specs=[blk(2 * LANES), per_b(2 * LANES), per_b(DIFF_W), full(lam), full(gsub)],
        out_specs=blk(DIFF_W),
        out_shape=jax.ShapeDtypeStruct((b, t, DIFF_W), BF16),
        compiler_params=sem,
    )(qd, kd, vd, lam, gsub)
    ml = pl.pallas_call(
        functools.partial(_mla_kernel, s_lat=s_lat, n_ctx=n_ctx),
        grid=(b, nj),
        in_specs=[blk(MLA_HEADS * SLOT), per_b(SLOT)],
        out_specs=blk(MLA_HEADS * LANES),
        out_shape=jax.ShapeDtypeStruct((b, t, MLA_HEADS * LANES), BF16),
        compiler_params=sem,
    )(qm, kvm)
    return na, df, ml


def _route(sel_t, n_tok):
    gsz = N_EXPERTS // N_GROUPS
    sub = lax.broadcasted_iota(jnp.int32, (gsz, n_tok), 0)
    grp = []
    for g in range(N_GROUPS):
        v = sel_t[g * gsz:(g + 1) * gsz, :]
        m1 = jnp.max(v, axis=0, keepdims=True)
        first = jnp.min(jnp.where(v == m1, sub, gsz), axis=0, keepdims=True)
        m2 = jnp.max(jnp.where(sub == first, -jnp.inf, v), axis=0, keepdims=True)
        grp.append(m1 + m2)
    rows = []
    for g in range(N_GROUPS):
        rank = jnp.zeros((1, n_tok), jnp.int32)
        for o in range(N_GROUPS):
            if o == g:
                continue
            ahead = (grp[o] > grp[g]) | ((grp[o] == grp[g]) & (o < g))
            rank = rank + ahead.astype(jnp.int32)
        keep = rank < TOPK_GROUPS
        rows.append(jnp.where(keep, sel_t[g * gsz:(g + 1) * gsz, :], -jnp.inf))
    v = jnp.concatenate(rows, axis=0)
    eio = lax.broadcasted_iota(jnp.int32, (N_EXPERTS, n_tok), 0)
    chosen = jnp.zeros((N_EXPERTS, n_tok), F32)
    for _ in range(TOP_K):
        m = jnp.max(v, axis=0, keepdims=True)
        idx = jnp.min(jnp.where(v == m, eio, N_EXPERTS), axis=0, keepdims=True)
        hit = eio == idx
        chosen = jnp.where(hit, 1.0, chosen)
        v = jnp.where(hit, -jnp.inf, v)
    return chosen


def _post_kernel(x_ref, na_ref, df_ref, ml_ref, mod_ref, wuv_ref, wout_ref, g_ref, rw_ref, rwt_ref, rb_ref,
                 x1_ref, h2_ref, gate_ref):
    mparts = [_dot(ml_ref[0, :, p * SLOT:(p + 1) * SLOT], wuv_ref[p]).astype(BF16) for p in range(MLA_HEADS // 2)]
    a = (_dot(na_ref[0], wout_ref[0:NA_W, :]) + _dot(df_ref[0], wout_ref[NA_W:NA_W + DIFF_W, :])
         + _dot(jnp.concatenate(mparts, axis=-1), wout_ref[NA_W + DIFF_W:, :]))
    x1 = x_ref[0] + mod_ref[0, 2:3, :] * a
    x1_ref[0] = x1
    h2 = _rms(x1) * g_ref[...]
    h2 = h2 * (1.0 + mod_ref[0, 4:5, :]) + mod_ref[0, 3:4, :]
    h2_ref[0] = h2.astype(BF16)

    h_hi, h_lo = _split(h2)
    w_hi, w_lo = _split(rw_ref[...])
    wt_hi, wt_lo = _split(rwt_ref[...])
    scores = jax.nn.sigmoid(_dot(h_hi, w_hi) + _dot(h_hi, w_lo) + _dot(h_lo, w_hi))
    sel_t = jax.nn.sigmoid(_dot_nt(wt_hi, h_hi) + _dot_nt(wt_lo, h_hi) + _dot_nt(wt_hi, h_lo)) + rb_ref[...]
    chosen_t = _route(sel_t, TQ).astype(BF16)
    eye = (lax.broadcasted_iota(jnp.int32, (TQ, TQ), 0) == lax.broadcasted_iota(jnp.int32, (TQ, TQ), 1))
    chosen = _dot_nt(eye.astype(F32).astype(BF16), chosen_t)
    w = scores * chosen
    gate_ref[0] = w * (ROUTED_SCALE / jnp.sum(w, axis=-1, keepdims=True))


def _post(x_all, na, df, ml, mod, wuv, wout, g, rw, rwt, rb, s_lat, nj):
    b, t, d = x_all.shape
    njl = s_lat // TQ
    blk = lambda w: pl.BlockSpec((1, TQ, w), lambda i, j: (i, j, 0))
    full = lambda a: pl.BlockSpec(a.shape, lambda i, j: (0,) * a.ndim)
    return pl.pallas_call(
        _post_kernel,
        grid=(b, nj),
        in_specs=[blk(d), blk(NA_W), blk(DIFF_W), blk(MLA_HEADS * LANES),
                  pl.BlockSpec((1, 6, d), lambda i, j: (jnp.where(j < njl, i, b), 0, 0)),
                  full(wuv), full(wout), full(g), full(rw), full(rwt), full(rb)],
        out_specs=[blk(d), blk(d), blk(N_EXPERTS)],
        out_shape=[jax.ShapeDtypeStruct((b, t, d), F32), jax.ShapeDtypeStruct((b, t, d), BF16),
                   jax.ShapeDtypeStruct((b, t, N_EXPERTS), F32)],
        compiler_params=_params(("arbitrary", "arbitrary")),
    )(x_all, na, df, ml, mod, wuv, wout, g, rw, rwt, rb)


def _moe_kernel(h_ref, gate_ref, x_ref, modl_ref, modc_ref, wgu_ref, wd_ref, sgu_ref, sd_ref, fg_ref,
                o_ref, acc_ref, act_ref, *, s_lat, tm, final):
    i2 = pl.program_id(1)
    e0 = pl.program_id(2)
    h = h_ref[0]
    ff = EXPERT_FF

    @pl.when(e0 == 0)
    def _():
        u = _dot(h, sgu_ref[...])
        acc_ref[...] = _dot((_silu(u[:, 0:ff]) * u[:, ff:2 * ff]).astype(BF16), sd_ref[...])

    gates = gate_ref[0]
    lane = lax.broadcasted_iota(jnp.int32, gates.shape, 1)
    for k in range(EXPERTS_PER_STEP):
        u = _dot(h, wgu_ref[k])
        gcol = jnp.sum(jnp.where(lane == e0 * EXPERTS_PER_STEP + k, gates, 0.0), axis=-1, keepdims=True)
        act_ref[:, k * ff:(k + 1) * ff] = (_silu(u[:, 0:ff]) * u[:, ff:2 * ff] * gcol).astype(BF16)
    acc_ref[...] += _dot(act_ref[...], wd_ref[0])

    @pl.when(e0 == pl.num_programs(2) - 1)
    def _():
        row = i2 * tm + lax.broadcasted_iota(jnp.int32, (tm, 1), 0)
        g2 = jnp.where(row < s_lat, modl_ref[0, 5:6, :], modc_ref[0, 5:6, :])
        y = x_ref[0] + g2 * acc_ref[...]
        if final:
            y = _rms(y) * fg_ref[...]
        o_ref[0] = y


def _moe(h2, gates, x1, mod, wgu, wd, sgu, sd, fg, s_lat, rows, final):
    b, t, d = x1.shape
    tm = rows // 2
    ne = wgu.shape[0] // EXPERTS_PER_STEP
    blk = lambda w: pl.BlockSpec((1, tm, w), lambda i, j, e: (i, j, 0))
    full = lambda a: pl.BlockSpec(a.shape, lambda i, j, e: (0,) * a.ndim)
    return pl.pallas_call(
        functools.partial(_moe_kernel, s_lat=s_lat, tm=tm, final=final),
        grid=(b, 2, ne),
        in_specs=[blk(d), blk(N_EXPERTS), blk(d),
                  pl.BlockSpec((1, 6, d), lambda i, j, e: (i, 0, 0)),
                  pl.BlockSpec((1, 6, d), lambda i, j, e: (b, 0, 0)),
                  pl.BlockSpec((EXPERTS_PER_STEP, d, 2 * EXPERT_FF), lambda i, j, e: (e, 0, 0)),
                  pl.BlockSpec((1, EXPERTS_PER_STEP * EXPERT_FF, d), lambda i, j, e: (e, 0, 0)),
                  full(sgu), full(sd), full(fg)],
        out_specs=blk(d),
        out_shape=jax.ShapeDtypeStruct((b, rows, d), F32),
        scratch_shapes=[pltpu.VMEM((tm, d), F32), pltpu.VMEM((tm, EXPERTS_PER_STEP * EXPERT_FF), BF16)],
        compiler_params=_params(("arbitrary", "arbitrary", "arbitrary")),
    )(h2, gates, x1, mod, mod, wgu, wd.reshape(ne, EXPERTS_PER_STEP * EXPERT_FF, d), sgu, sd, fg)


def _w_in_columns():
    z = IN_COLS
    cols = list(range(0, 3 * NA_W))
    for base in (O_DIFF, O_DIFF + DIFF_W):
        for half in range(2):
            for m in range(2 * DIFF_HEADS):
                cols += [base + m * DIFF_QK_DIM + half * HALF + i for i in range(HALF)]
    cols += list(range(O_DIFF + 2 * DIFF_W, O_MLA + MLA_Q_RANK + MLA_KV_RANK))
    kr = O_MLA + MLA_Q_RANK + MLA_KV_RANK
    cols += list(range(kr, kr + MLA_ROPE)) + [z] * (LANES - MLA_ROPE)
    rot = list(range(kr + HALF, kr + MLA_ROPE)) + list(range(kr, kr + HALF)) + [z] * (LANES - MLA_ROPE)
    sign = np.ones((W_IN_COLS,), np.float32)
    sign[len(cols):len(cols) + HALF] = -1.0
    cols += rot
    assert len(cols) == W_IN_COLS
    return np.asarray(cols, np.int32), sign


def _w_uq_columns():
    per = MLA_NOPE + MLA_ROPE
    cols = [h * per + i for h in range(MLA_HEADS) for i in range(MLA_NOPE)]
    for half in range(2):
        cols += [h * per + MLA_NOPE + half * HALF + i for h in range(MLA_HEADS) for i in range(HALF)]
    return np.asarray(cols, np.int32)


def _layer_weights(w_in, w_uq, w_ukv, w_out):
    cols, sign = _w_in_columns()
    win = (jnp.concatenate([w_in, jnp.zeros((w_in.shape[0], 1), F32)], axis=1)[:, cols] * sign).astype(BF16)
    wuq = w_uq[:, _w_uq_columns()].astype(BF16)
    kv = w_ukv.reshape(MLA_KV_RANK, MLA_HEADS, MLA_NOPE + MLA_V)
    wq2 = jnp.zeros((MLA_HEADS * MLA_NOPE + 2 * LANES, MLA_HEADS * SLOT), F32)
    place = np.zeros((2 * LANES, MLA_HEADS * SLOT), np.float32)
    for h in range(MLA_HEADS):
        wq2 = wq2.at[h * MLA_NOPE:(h + 1) * MLA_NOPE, h * SLOT:h * SLOT + MLA_KV_RANK].set(kv[:, h, :MLA_NOPE].T)
        for i in range(HALF):
            place[h * HALF + i, h * SLOT + MLA_KV_RANK + i] = 1.0
            place[LANES + h * HALF + i, h * SLOT + MLA_KV_RANK + HALF + i] = 1.0
    wq2 = wq2.at[MLA_HEADS * MLA_NOPE:, :].set(place).astype(BF16)
    wuv = jnp.zeros((MLA_HEADS // 2, SLOT, 2 * MLA_V), F32)
    for h in range(MLA_HEADS):
        wuv = wuv.at[h // 2, (h % 2) * LANES:(h % 2 + 1) * LANES, (h % 2) * MLA_V:(h % 2 + 1) * MLA_V].set(
            kv[:, h, MLA_NOPE:])
    return win, wuq, wq2, wuv.astype(BF16), w_out.astype(BF16)


def _rope_tables(s_lat, n_ctx):
    t = np.arange(s_lat)
    n_freq = ROPE_DIM // 4
    freq = ROPE_THETA ** (-jnp.arange(n_freq, dtype=F32) / n_freq)
    ang = jnp.concatenate([(t // GRID_W).astype(np.float32)[:, None] * freq,
                           (t % GRID_W).astype(np.float32)[:, None] * freq], axis=-1)
    cos, sin = jnp.cos(ang), jnp.sin(ang)
    one, zero = jnp.ones((n_ctx, LANES), F32), jnp.zeros((n_ctx, LANES), F32)
    rep = LANES // HALF
    cosd = jnp.concatenate([jnp.tile(cos, (1, rep)), one], axis=0)
    sind = jnp.concatenate([jnp.tile(sin, (1, rep)), zero], axis=0)
    return cosd, sind, cosd, sind


def _na_bias(rel_bias, s_lat):
    qr = np.arange(TQ) // GRID_W
    qc = np.arange(TQ) % GRID_W
    kr = np.arange(NA_WIN) // GRID_W
    kc = np.arange(NA_WIN) % GRID_W
    cs = np.clip(qc - WIN_C // 2, 0, GRID_W - WIN_C)
    col_ok = (kc[None, :] >= cs[:, None]) & (kc[None, :] < cs[:, None] + WIN_C)
    dc = np.clip(kc[None, :] - qc[:, None] + WIN_C - 1, 0, 2 * WIN_C - 2)
    out = []
    for delta, lo in ((0, 0), (-(WIN_R // 2), None), (-WIN_R, WIN_R // 2)):
        rel = delta + kr[None, :] - qr[:, None]
        if lo is None:
            row_ok = (rel >= -(WIN_R // 2)) & (rel < WIN_R // 2)
        else:
            row_ok = (kr[None, :] >= lo) & (kr[None, :] < lo + WIN_R) & np.ones_like(rel, bool)
        dr = np.clip(rel + WIN_R - 1, 0, 2 * WIN_R - 2)
        out.append(jnp.where((row_ok & col_ok)[None], rel_bias[:, dr, dc], NEG))
    return jnp.stack(out, axis=0)


def kernel(x, c, ctx, c_ctx, w_ada, b_ada, norm_attn_g, norm_ffn_g, w_in, na_rel_bias, diff_lambda, diff_subln_g, mla_q_norm_g, mla_w_uq, mla_kv_norm_g, mla_w_ukv, w_out, router_w, router_bias, expert_w_gate, expert_w_up, expert_w_down, shared_w_gate, shared_w_up, shared_w_down, final_norm_g):
    b, s_lat, d = x.shape
    n_ctx = ctx.shape[1]
    assert d == D_MODEL and n_ctx == TQ and s_lat % TQ == 0 and s_lat // GRID_W >= NA_WIN_ROWS + Q_ROWS
    t = s_lat + n_ctx

    c_rows = 32
    c_all = jnp.concatenate([c, c_ctx[None, :], jnp.zeros((c_rows - b - 1, d), F32)], axis=0)
    mod = _ada_mod(c_all, w_ada, b_ada)[:, :b + 1].reshape(DEPTH, b + 1, 6, d)

    tabs = _rope_tables(s_lat, n_ctx)
    row = lambda v: v.reshape(1, -1)
    xa = jnp.concatenate([x, ctx], axis=1)
    out = None
    for l in range(DEPTH):
        need_ctx = l < DEPTH - 1
        nj = t // TQ if need_ctx else s_lat // TQ
        lambda_init = 0.8 - 0.6 * math.exp(-0.3 * l)
        win, wuq, wq2, wuv, wout = _layer_weights(w_in[l], mla_w_uq[l], mla_w_ukv[l], w_out[l])
        qn, kn, vn, qd, kd, vd, qm, kvm = _pre(
            xa, mod[l], row(norm_attn_g[l]), win, row(mla_q_norm_g[l]), wuq, row(mla_kv_norm_g[l]), wq2, tabs, s_lat)
        gsub = row(jnp.tile(diff_subln_g[l], DIFF_HEADS) * (1.0 - lambda_init))
        na, df, ml = _attention(qn, kn, vn, _na_bias(na_rel_bias[l], s_lat), qd, kd, vd, diff_lambda[l], gsub,
                                qm, kvm, s_lat, n_ctx, nj, lambda_init)
        rb = jnp.broadcast_to(router_bias[l][:, None], (N_EXPERTS, TQ))
        x1, h2, gates = _post(xa, na, df, ml, mod[l], wuv, wout, row(norm_ffn_g[l]), router_w[l], router_w[l].T, rb,
                              s_lat, nj)
        wgu = jnp.concatenate([expert_w_gate[l], expert_w_up[l]], axis=-1).astype(BF16)
        sgu = jnp.concatenate([shared_w_gate[l], shared_w_up[l]], axis=-1).astype(BF16)
        res = _moe(h2, gates, x1, mod[l], wgu, expert_w_down[l].astype(BF16), sgu, shared_w_down[l].astype(BF16),
                   row(final_norm_g), s_lat, t if need_ctx else s_lat, final=not need_ctx)
        if need_ctx:
            xa = res
        else:
            out = res
    return out
```

```python
import functools
import math

import numpy as np
import jax
import jax.numpy as jnp
from jax import lax
from jax.experimental import pallas as pl
from jax.experimental.pallas import tpu as pltpu

F32 = jnp.float32
BF16 = jnp.bfloat16

D_MODEL = 1024
DEPTH = 2
GRID_W = 64
EPS = 1e-6
NA_HEADS, NA_DIM, WIN_R, WIN_C = 4, 64, 8, 16
DIFF_HEADS, DIFF_QK_DIM, DIFF_V_DIM = 4, 32, 64
MLA_HEADS, MLA_NOPE, MLA_ROPE, MLA_V = 8, 64, 32, 64
MLA_Q_RANK, MLA_KV_RANK = 256, 128
ROPE_DIM = 32
ROPE_THETA = 10000.0
N_EXPERTS, TOP_K, N_GROUPS, TOPK_GROUPS = 64, 8, 8, 4
EXPERT_FF = 256
ROUTED_SCALE = 2.5

NA_W = NA_HEADS * NA_DIM
DIFF_W = DIFF_HEADS * DIFF_V_DIM
MLA_W = MLA_HEADS * MLA_V
O_DIFF = 3 * NA_W
O_MLA = O_DIFF + 3 * DIFF_W
IN_COLS = O_MLA + MLA_Q_RANK + MLA_KV_RANK + MLA_ROPE
HALF = ROPE_DIM // 2

LANES = 128
TQ = 256
Q_ROWS = TQ // GRID_W
NA_WIN_ROWS = Q_ROWS + WIN_R
NA_WIN = NA_WIN_ROWS * GRID_W
SLOT = 2 * LANES
W_IN_COLS = 3 * NA_W + 3 * DIFF_W + MLA_Q_RANK + 3 * LANES
EXPERTS_PER_STEP = 4
NEG = -1e30
LOG2E = math.log2(math.e)
VMEM_LIMIT = 56 * 1024 * 1024


def _dot(a, b):
    return jnp.dot(a, b, preferred_element_type=F32)


def _dot_nt(a, b):
    return lax.dot_general(a, b, (((1,), (1,)), ((), ())), preferred_element_type=F32)


def _split(a):
    hi = a.astype(BF16)
    lo = (a - hi.astype(F32)).astype(BF16)
    return hi, lo


def _rms(x):
    return x * lax.rsqrt(jnp.mean(x * x, axis=-1, keepdims=True) + EPS)


def _silu(x):
    return x * jax.nn.sigmoid(x)


def _params(sem):
    return pltpu.CompilerParams(dimension_semantics=sem, vmem_limit_bytes=VMEM_LIMIT)


def _ada_kernel(c_ref, w_ref, b_ref, o_ref):
    a_hi, a_lo = _split(_silu(c_ref[...]))
    w_hi, w_lo = _split(w_ref[0])
    o_ref[0] = _dot(a_hi, w_hi) + _dot(a_hi, w_lo) + _dot(a_lo, w_hi) + b_ref[0]


def _ada_mod(c_all, w_ada, b_ada):
    depth, d, n = w_ada.shape
    rows = c_all.shape[0]
    nb = 1536
    return pl.pallas_call(
        _ada_kernel,
        grid=(depth, n // nb),
        in_specs=[pl.BlockSpec((rows, d), lambda l, j: (0, 0)),
                  pl.BlockSpec((1, d, nb), lambda l, j: (l, 0, j)),
                  pl.BlockSpec((1, 1, nb), lambda l, j: (l, 0, j))],
        out_specs=pl.BlockSpec((1, rows, nb), lambda l, j: (l, 0, j)),
        out_shape=jax.ShapeDtypeStruct((depth, rows, n), F32),
        compiler_params=_params(("arbitrary", "arbitrary")),
    )(c_all, w_ada, b_ada.reshape(depth, 1, n))


def _pre_kernel(x_ref, mod_ref, g_ref, win_ref, qg_ref, wuq_ref, kvg_ref, wq2_ref,
                cosd_ref, sind_ref, cosk_ref, sink_ref,
                qn_ref, kn_ref, vn_ref, qd_ref, kd_ref, vd_ref, qm_ref, kvm_ref):
    h = _rms(x_ref[0]) * g_ref[...]
    h = h * (1.0 + mod_ref[0, 1:2, :]) + mod_ref[0, 0:1, :]
    p = _dot(h.astype(BF16), win_ref[...])
    cd, sd = cosd_ref[...], sind_ref[...]

    qn_ref[0] = (p[:, 0:256] * (NA_DIM ** -0.5 * LOG2E)).astype(BF16)
    kn_ref[0] = p[:, 256:512].astype(BF16)
    vn_ref[0] = p[:, 512:768].astype(BF16)

    qa, qb = p[:, 768:896], p[:, 896:1024]
    sc = DIFF_QK_DIM ** -0.5 * LOG2E
    qd_ref[0, :, 0:LANES] = ((qa * cd - qb * sd) * sc).astype(BF16)
    qd_ref[0, :, LANES:2 * LANES] = ((qa * sd + qb * cd) * sc).astype(BF16)
    ka, kb = p[:, 1024:1152], p[:, 1152:1280]
    kd_ref[0, :, 0:LANES] = (ka * cd - kb * sd).astype(BF16)
    kd_ref[0, :, LANES:2 * LANES] = (ka * sd + kb * cd).astype(BF16)
    vd_ref[0] = p[:, 1280:1536].astype(BF16)

    cq = (_rms(p[:, 1536:1792]) * qg_ref[...]).astype(BF16)
    q1 = _dot(cq, wuq_ref[...])
    ms = (MLA_NOPE + MLA_ROPE) ** -0.5 * LOG2E
    ra, rb = q1[:, 512:640], q1[:, 640:768]
    q2 = jnp.concatenate([(q1[:, 0:512] * ms).astype(BF16),
                          ((ra * cd - rb * sd) * ms).astype(BF16),
                          ((ra * sd + rb * cd) * ms).astype(BF16)], axis=-1)
    qm_ref[0] = _dot(q2, wq2_ref[...]).astype(BF16)

    kvm_ref[0, :, 0:LANES] = (_rms(p[:, 1792:1920]) * kvg_ref[...]).astype(BF16)
    kvm_ref[0, :, LANES:2 * LANES] = (p[:, 1920:2048] * cosk_ref[...] + p[:, 2048:2176] * sink_ref[...]).astype(BF16)


def _pre(x_all, mod, g, win, qg, wuq, kvg, wq2, tabs, s_lat):
    b, t, d = x_all.shape
    nj = t // TQ
    njl = s_lat // TQ
    tok = lambda w: pl.BlockSpec((1, TQ, w), lambda i, j: (i, j, 0))
    full = lambda a: pl.BlockSpec(a.shape, lambda i, j: (0,) * a.ndim)
    tab = pl.BlockSpec((TQ, LANES), lambda i, j: (j, 0))
    widths = [NA_W, NA_W, NA_W, 2 * LANES, 2 * LANES, DIFF_W, MLA_HEADS * SLOT, SLOT]
    return pl.pallas_call(
        _pre_kernel,
        grid=(b, nj),
        in_specs=[tok(d),
                  pl.BlockSpec((1, 6, d), lambda i, j: (jnp.where(j < njl, i, b), 0, 0)),
                  full(g), full(win), full(qg), full(wuq), full(kvg), full(wq2), tab, tab, tab, tab],
        out_specs=[tok(w) for w in widths],
        out_shape=[jax.ShapeDtypeStruct((b, t, w), BF16) for w in widths],
        compiler_params=_params(("arbitrary", "arbitrary")),
    )(x_all, mod, g, win, qg, wuq, kvg, wq2, *tabs)


def _lane_mask(shape, lo, width):
    lane = lax.broadcasted_iota(jnp.int32, shape, len(shape) - 1)
    return (lane >= lo) & (lane < lo + width)


def _na_kernel(q_ref, k_ref, v_ref, bias_ref, o_ref, *, s_lat, n_ctx):
    j = pl.program_id(1)
    njl = s_lat // TQ
    rows = s_lat // GRID_W
    q = q_ref[0]
    kc = k_ref[0, s_lat:s_lat + n_ctx, :]
    vc = v_ref[0, s_lat:s_lat + n_ctx, :]

    def heads(latent):
        if latent:
            w0 = jnp.clip(j * Q_ROWS - WIN_R // 2, 0, rows - NA_WIN_ROWS)
            start = pl.multiple_of(w0 * GRID_W, GRID_W)
            kw = k_ref[0, pl.ds(start, NA_WIN), :]
            vw = v_ref[0, pl.ds(start, NA_WIN), :]
        acc = jnp.zeros((TQ, NA_W), F32)
        for h in range(NA_HEADS):
            mh = _lane_mask((TQ, NA_W), h * NA_DIM, NA_DIM)
            qh = jnp.where(mh, q, jnp.zeros_like(q))
            s2 = _dot_nt(qh, kc)
            m = jnp.max(s2, axis=-1, keepdims=True)
            if latent:
                s1 = _dot_nt(qh, kw) + bias_ref[0, h]
                m = jnp.maximum(m, jnp.max(s1, axis=-1, keepdims=True))
                e1 = jnp.exp2(s1 - m)
            e2 = jnp.exp2(s2 - m)
            l = jnp.sum(e2, axis=-1, keepdims=True)
            o = _dot(e2.astype(BF16), vc)
            if latent:
                l = l + jnp.sum(e1, axis=-1, keepdims=True)
                o = o + _dot(e1.astype(BF16), vw)
            acc = acc + jnp.where(mh, o * (1.0 / l), 0.0)
        o_ref[0] = acc.astype(BF16)

    @pl.when(j < njl)
    def _():
        heads(True)

    @pl.when(j >= njl)
    def _():
        heads(False)


def _diff_kernel(q_ref, k_ref, v_ref, lam_ref, gsub_ref, o_ref, *, s_lat, n_ctx, lambda_init):
    j = pl.program_id(1)
    njl = s_lat // TQ
    q = q_ref[0]
    lv = lam_ref[...]
    lam = (jnp.exp(jnp.sum(lv[0:1] * lv[1:2], axis=-1, keepdims=True))
           - jnp.exp(jnp.sum(lv[2:3] * lv[3:4], axis=-1, keepdims=True)) + lambda_init)

    def heads(k, v):
        acc = jnp.zeros((TQ, DIFF_W), F32)
        for h in range(DIFF_HEADS):
            es, ls = [], []
            for m in range(2):
                lo = (2 * h + m) * HALF
                mm = _lane_mask((TQ, 2 * LANES), lo, HALF) | _lane_mask((TQ, 2 * LANES), LANES + lo, HALF)
                s = _dot_nt(jnp.where(mm, q, jnp.zeros_like(q)), k)
                e = jnp.exp2(s - jnp.max(s, axis=-1, keepdims=True))
                es.append(e)
                ls.append(jnp.sum(e, axis=-1, keepdims=True))
            a = es[0] * (1.0 / ls[0]) - es[1] * (lam / ls[1])
            o = _dot(a.astype(BF16), v)
            om = jnp.where(_lane_mask((TQ, DIFF_W), h * DIFF_V_DIM, DIFF_V_DIM), o, 0.0)
            ms = jnp.sum(om * om, axis=-1, keepdims=True) * (1.0 / DIFF_V_DIM)
            acc = acc + om * lax.rsqrt(ms + EPS)
        o_ref[0] = (acc * gsub_ref[...]).astype(BF16)

    @pl.when(j < njl)
    def _():
        heads(k_ref[0], v_ref[0])

    @pl.when(j >= njl)
    def _():
        heads(k_ref[0, s_lat:s_lat + n_ctx, :], v_ref[0, s_lat:s_lat + n_ctx, :])


def _mla_kernel(q_ref, kv_ref, o_ref, *, s_lat, n_ctx):
    j = pl.program_id(1)
    njl = s_lat // TQ

    def heads(kv):
        v = kv[:, 0:LANES]
        for h in range(MLA_HEADS):
            s = _dot_nt(q_ref[0, :, h * SLOT:(h + 1) * SLOT], kv)
            e = jnp.exp2(s - jnp.max(s, axis=-1, keepdims=True))
            l = jnp.sum(e, axis=-1, keepdims=True)
            o_ref[0, :, h * LANES:(h + 1) * LANES] = (_dot(e.astype(BF16), v) * (1.0 / l)).astype(BF16)

    @pl.when(j < njl)
    def _():
        heads(kv_ref[0])

    @pl.when(j >= njl)
    def _():
        heads(kv_ref[0, s_lat:s_lat + n_ctx, :])


def _attention(qn, kn, vn, bias, qd, kd, vd, lam, gsub, qm, kvm, s_lat, n_ctx, nj, lambda_init):
    b, t, _ = qn.shape
    njl = s_lat // TQ
    blk = lambda w: pl.BlockSpec((1, TQ, w), lambda i, j: (i, j, 0))
    per_b = lambda w: pl.BlockSpec((1, t, w), lambda i, j: (i, 0, 0))
    full = lambda a: pl.BlockSpec(a.shape, lambda i, j: (0,) * a.ndim)
    sem = _params(("arbitrary", "arbitrary"))
    variant = lambda i, j: (jnp.where(j == 0, 0, jnp.where(j >= njl - 1, 2, 1)), 0, 0, 0)
    na = pl.pallas_call(
        functools.partial(_na_kernel, s_lat=s_lat, n_ctx=n_ctx),
        grid=(b, nj),
        in_specs=[blk(NA_W), per_b(NA_W), per_b(NA_W),
                  pl.BlockSpec((1, NA_HEADS, TQ, NA_WIN), variant)],
        out_specs=blk(NA_W),
        out_shape=jax.ShapeDtypeStruct((b, t, NA_W), BF16),
        compiler_params=sem,
    )(qn, kn, vn, bias)
    df = pl.pallas_call(
        functools.partial(_diff_kernel, s_lat=s_lat, n_ctx=n_ctx, lambda_init=lambda_init),
        grid=(b, nj),
        in_specs=[blk(2 * LANES), per_b(2 * LANES), per_b(DIFF_W), full(lam), full(gsub)],
        out_specs=blk(DIFF_W),
        out_shape=jax.ShapeDtypeStruct((b, t, DIFF_W), BF16),
        compiler_params=sem,
    )(qd, kd, vd, lam, gsub)
    ml = pl.pallas_call(
        functools.partial(_mla_kernel, s_lat=s_lat, n_ctx=n_ctx),
        grid=(b, nj),
        in_specs=[blk(MLA_HEADS * SLOT), per_b(SLOT)],
        out_specs=blk(MLA_HEADS * LANES),
        out_shape=jax.ShapeDtypeStruct((b, t, MLA_HEADS * LANES), BF16),
        compiler_params=sem,
    )(qm, kvm)
    return na, df, ml


def _route(sel_t, n_tok):
    gsz = N_EXPERTS // N_GROUPS
    sub = lax.broadcasted_iota(jnp.int32, (gsz, n_tok), 0)
    grp = []
    for g in range(N_GROUPS):
        v = sel_t[g * gsz:(g + 1) * gsz, :]
        m1 = jnp.max(v, axis=0, keepdims=True)
        first = jnp.min(jnp.where(v == m1, sub, gsz), axis=0, keepdims=True)
        m2 = jnp.max(jnp.where(sub == first, -jnp.inf, v), axis=0, keepdims=True)
        grp.append(m1 + m2)
    rows = []
    for g in range(N_GROUPS):
        rank = jnp.zeros((1, n_tok), jnp.int32)
        for o in range(N_GROUPS):
            if o == g:
                continue
            ahead = (grp[o] > grp[g]) | ((grp[o] == grp[g]) & (o < g))
            rank = rank + ahead.astype(jnp.int32)
        keep = rank < TOPK_GROUPS
        rows.append(jnp.where(keep, sel_t[g * gsz:(g + 1) * gsz, :], -jnp.inf))
    v = jnp.concatenate(rows, axis=0)
    eio = lax.broadcasted_iota(jnp.int32, (N_EXPERTS, n_tok), 0)
    chosen = jnp.zeros((N_EXPERTS, n_tok), F32)
    for _ in range(TOP_K):
        m = jnp.max(v, axis=0, keepdims=True)
        idx = jnp.min(jnp.where(v == m, eio, N_EXPERTS), axis=0, keepdims=True)
        hit = eio == idx
        chosen = jnp.where(hit, 1.0, chosen)
        v = jnp.where(hit, -jnp.inf, v)
    return chosen


def _post_kernel(x_ref, na_ref, df_ref, ml_ref, mod_ref, wuv_ref, wout_ref, g_ref, rw_ref, rwt_ref, rb_ref,
                 x1_ref, h2_ref, gate_ref):
    mparts = [_dot(ml_ref[0, :, p * SLOT:(p + 1) * SLOT], wuv_ref[p]).astype(BF16) for p in range(MLA_HEADS // 2)]
    a = (_dot(na_ref[0], wout_ref[0:NA_W, :]) + _dot(df_ref[0], wout_ref[NA_W:NA_W + DIFF_W, :])
         + _dot(jnp.concatenate(mparts, axis=-1), wout_ref[NA_W + DIFF_W:, :]))
    x1 = x_ref[0] + mod_ref[0, 2:3, :] * a
    x1_ref[0] = x1
    h2 = _rms(x1) * g_ref[...]
    h2 = h2 * (1.0 + mod_ref[0, 4:5, :]) + mod_ref[0, 3:4, :]
    h2_ref[0] = h2.astype(BF16)

    h_hi, h_lo = _split(h2)
    w_hi, w_lo = _split(rw_ref[...])
    wt_hi, wt_lo = _split(rwt_ref[...])
    scores = jax.nn.sigmoid(_dot(h_hi, w_hi) + _dot(h_hi, w_lo) + _dot(h_lo, w_hi))
    sel_t = jax.nn.sigmoid(_dot_nt(wt_hi, h_hi) + _dot_nt(wt_lo, h_hi) + _dot_nt(wt_hi, h_lo)) + rb_ref[...]
    chosen_t = _route(sel_t, TQ).astype(BF16)
    eye = (lax.broadcasted_iota(jnp.int32, (TQ, TQ), 0) == lax.broadcasted_iota(jnp.int32, (TQ, TQ), 1))
    chosen = _dot_nt(eye.astype(F32).astype(BF16), chosen_t)
    w = scores * chosen
    gate_ref[0] = w * (ROUTED_SCALE / jnp.sum(w, axis=-1, keepdims=True))


def _post(x_all, na, df, ml, mod, wuv, wout, g, rw, rwt, rb, s_lat, nj):
    b, t, d = x_all.shape
    njl = s_lat // TQ
    blk = lambda w: pl.BlockSpec((1, TQ, w), lambda i, j: (i, j, 0))
    full = lambda a: pl.BlockSpec(a.shape, lambda i, j: (0,) * a.ndim)
    return pl.pallas_call(
        _post_kernel,
        grid=(b, nj),
        in_specs=[blk(d), blk(NA_W), blk(DIFF_W), blk(MLA_HEADS * LANES),
                  pl.BlockSpec((1, 6, d), lambda i, j: (jnp.where(j < njl, i, b), 0, 0)),
                  full(wuv), full(wout), full(g), full(rw), full(rwt), full(rb)],
        out_specs=[blk(d), blk(d), blk(N_EXPERTS)],
        out_shape=[jax.ShapeDtypeStruct((b, t, d), F32), jax.ShapeDtypeStruct((b, t, d), BF16),
                   jax.ShapeDtypeStruct((b, t, N_EXPERTS), F32)],
        compiler_params=_params(("arbitrary", "arbitrary")),
    )(x_all, na, df, ml, mod, wuv, wout, g, rw, rwt, rb)


def _moe_kernel(h_ref, gate_ref, x_ref, modl_ref, modc_ref, wgu_ref, wd_ref, sgu_ref, sd_ref, fg_ref,
                o_ref, acc_ref, act_ref, *, s_lat, tm, final):
    i2 = pl.program_id(1)
    e0 = pl.program_id(2)
    h = h_ref[0]
    ff = EXPERT_FF

    @pl.when(e0 == 0)
    def _():
        u = _dot(h, sgu_ref[...])
        acc_ref[...] = _dot((_silu(u[:, 0:ff]) * u[:, ff:2 * ff]).astype(BF16), sd_ref[...])

    gates = gate_ref[0]
    lane = lax.broadcasted_iota(jnp.int32, gates.shape, 1)
    for k in range(EXPERTS_PER_STEP):
        u = _dot(h, wgu_ref[k])
        gcol = jnp.sum(jnp.where(lane == e0 * EXPERTS_PER_STEP + k, gates, 0.0), axis=-1, keepdims=True)
        act_ref[:, k * ff:(k + 1) * ff] = (_silu(u[:, 0:ff]) * u[:, ff:2 * ff] * gcol).astype(BF16)
    acc_ref[...] += _dot(act_ref[...], wd_ref[0])

    @pl.when(e0 == pl.num_programs(2) - 1)
    def _():
        row = i2 * tm + lax.broadcasted_iota(jnp.int32, (tm, 1), 0)
        g2 = jnp.where(row < s_lat, modl_ref[0, 5:6, :], modc_ref[0, 5:6, :])
        y = x_ref[0] + g2 * acc_ref[...]
        if final:
            y = _rms(y) * fg_ref[...]
        o_ref[0] = y


def _moe(h2, gates, x1, mod, wgu, wd, sgu, sd, fg, s_lat, rows, final):
    b, t, d = x1.shape
    tm = rows // 2
    ne = wgu.shape[0] // EXPERTS_PER_STEP
    blk = lambda w: pl.BlockSpec((1, tm, w), lambda i, j, e: (i, j, 0))
    full = lambda a: pl.BlockSpec(a.shape, lambda i, j, e: (0,) * a.ndim)
    return pl.pallas_call(
        functools.partial(_moe_kernel, s_lat=s_lat, tm=tm, final=final),
        grid=(b, 2, ne),
        in_specs=[blk(d), blk(N_EXPERTS), blk(d),
                  pl.BlockSpec((1, 6, d), lambda i, j, e: (i, 0, 0)),
                  pl.BlockSpec((1, 6, d), lambda i, j, e: (b, 0, 0)),
                  pl.BlockSpec((EXPERTS_PER_STEP, d, 2 * EXPERT_FF), lambda i, j, e: (e, 0, 0)),
                  pl.BlockSpec((1, EXPERTS_PER_STEP * EXPERT_FF, d), lambda i, j, e: (e, 0, 0)),
                  full(sgu), full(sd), full(fg)],
        out_specs=blk(d),
        out_shape=jax.ShapeDtypeStruct((b, rows, d), F32),
        scratch_shapes=[pltpu.VMEM((tm, d), F32), pltpu.VMEM((tm, EXPERTS_PER_STEP * EXPERT_FF), BF16)],
        compiler_params=_params(("arbitrary", "arbitrary", "arbitrary")),
    )(h2, gates, x1, mod, mod, wgu, wd.reshape(ne, EXPERTS_PER_STEP * EXPERT_FF, d), sgu, sd, fg)


def _w_in_columns():
    z = IN_COLS
    cols = list(range(0, 3 * NA_W))
    for base in (O_DIFF, O_DIFF + DIFF_W):
        for half in range(2):
            for m in range(2 * DIFF_HEADS):
                cols += [base + m * DIFF_QK_DIM + half * HALF + i for i in range(HALF)]
    cols += list(range(O_DIFF + 2 * DIFF_W, O_MLA + MLA_Q_RANK + MLA_KV_RANK))
    kr = O_MLA + MLA_Q_RANK + MLA_KV_RANK
    cols += list(range(kr, kr + MLA_ROPE)) + [z] * (LANES - MLA_ROPE)
    rot = list(range(kr + HALF, kr + MLA_ROPE)) + list(range(kr, kr + HALF)) + [z] * (LANES - MLA_ROPE)
    sign = np.ones((W_IN_COLS,), np.float32)
    sign[len(cols):len(cols) + HALF] = -1.0
    cols += rot
    assert len(cols) == W_IN_COLS
    return np.asarray(cols, np.int32), sign


def _w_uq_columns():
    per = MLA_NOPE + MLA_ROPE
    cols = [h * per + i for h in range(MLA_HEADS) for i in range(MLA_NOPE)]
    for half in range(2):
        cols += [h * per + MLA_NOPE + half * HALF + i for h in range(MLA_HEADS) for i in range(HALF)]
    return np.asarray(cols, np.int32)


def _layer_weights(w_in, w_uq, w_ukv, w_out):
    cols, sign = _w_in_columns()
    win = (jnp.concatenate([w_in, jnp.zeros((w_in.shape[0], 1), F32)], axis=1)[:, cols] * sign).astype(BF16)
    wuq = w_uq[:, _w_uq_columns()].astype(BF16)
    kv = w_ukv.reshape(MLA_KV_RANK, MLA_HEADS, MLA_NOPE + MLA_V)
    wq2 = jnp.zeros((MLA_HEADS * MLA_NOPE + 2 * LANES, MLA_HEADS * SLOT), F32)
    place = np.zeros((2 * LANES, MLA_HEADS * SLOT), np.float32)
    for h in range(MLA_HEADS):
        wq2 = wq2.at[h * MLA_NOPE:(h + 1) * MLA_NOPE, h * SLOT:h * SLOT + MLA_KV_RANK].set(kv[:, h, :MLA_NOPE].T)
        for i in range(HALF):
            place[h * HALF + i, h * SLOT + MLA_KV_RANK + i] = 1.0
            place[LANES + h * HALF + i, h * SLOT + MLA_KV_RANK + HALF + i] = 1.0
    wq2 = wq2.at[MLA_HEADS * MLA_NOPE:, :].set(place).astype(BF16)
    wuv = jnp.zeros((MLA_HEADS // 2, SLOT, 2 * MLA_V), F32)
    for h in range(MLA_HEADS):
        wuv = wuv.at[h // 2, (h % 2) * LANES:(h % 2 + 1) * LANES, (h % 2) * MLA_V:(h % 2 + 1) * MLA_V].set(
            kv[:, h, MLA_NOPE:])
    return win, wuq, wq2, wuv.astype(BF16), w_out.astype(BF16)


def _rope_tables(s_lat, n_ctx):
    t = np.arange(s_lat)
    n_freq = ROPE_DIM // 4
    freq = ROPE_THETA ** (-jnp.arange(n_freq, dtype=F32) / n_freq)
    ang = jnp.concatenate([(t // GRID_W).astype(np.float32)[:, None] * freq,
                           (t % GRID_W).astype(np.float32)[:, None] * freq], axis=-1)
    cos, sin = jnp.cos(ang), jnp.sin(ang)
    one, zero = jnp.ones((n_ctx, LANES), F32), jnp.zeros((n_ctx, LANES), F32)
    rep = LANES // HALF
    cosd = jnp.concatenate([jnp.tile(cos, (1, rep)), one], axis=0)
    sind = jnp.concatenate([jnp.tile(sin, (1, rep)), zero], axis=0)
    return cosd, sind, cosd, sind


def _na_bias(rel_bias, s_lat):
    qc = np.arange(GRID_W)
    cs = np.clip(qc - WIN_C // 2, 0, GRID_W - WIN_C)
    col_ok = (qc[None, :] >= cs[:, None]) & (qc[None, :] < cs[:, None] + WIN_C)
    dc = qc[None, :] - qc[:, None] + WIN_C - 1
    pick = (dc[:, :, None] == np.arange(2 * WIN_C - 1)[None, None, :]) & col_ok[:, :, None]
    tiles = jnp.einsum('hij,cdj->hicd', rel_bias, jnp.asarray(pick, F32), precision=lax.Precision.HIGHEST)
    tiles = jnp.where(col_ok[None, None], tiles * LOG2E, NEG)
    masked = jnp.full((NA_HEADS, GRID_W, GRID_W), NEG, F32)
    out = []
    for delta, lo in ((0, 0), (-(WIN_R // 2), None), (-WIN_R, WIN_R // 2)):
        rows = []
        for qr in range(Q_ROWS):
            row = []
            for kr in range(NA_WIN_ROWS):
                rel = delta + kr - qr
                ok = (-(WIN_R // 2) <= rel < WIN_R // 2) if lo is None else (lo <= kr < lo + WIN_R)
                row.append(tiles[:, rel + WIN_R - 1] if ok else masked)
            rows.append(jnp.concatenate(row, axis=-1))
        out.append(jnp.concatenate(rows, axis=-2))
    return jnp.stack(out, axis=0)


def kernel(x, c, ctx, c_ctx, w_ada, b_ada, norm_attn_g, norm_ffn_g, w_in, na_rel_bias, diff_lambda, diff_subln_g, mla_q_norm_g, mla_w_uq, mla_kv_norm_g, mla_w_ukv, w_out, router_w, router_bias, expert_w_gate, expert_w_up, expert_w_down, shared_w_gate, shared_w_up, shared_w_down, final_norm_g):
    b, s_lat, d = x.shape
    n_ctx = ctx.shape[1]
    assert d == D_MODEL and n_ctx == TQ and s_lat % TQ == 0 and s_lat // GRID_W >= NA_WIN_ROWS + Q_ROWS
    t = s_lat + n_ctx

    c_rows = 32
    c_all = jnp.concatenate([c, c_ctx[None, :], jnp.zeros((c_rows - b - 1, d), F32)], axis=0)
    mod = _ada_mod(c_all, w_ada, b_ada)[:, :b + 1].reshape(DEPTH, b + 1, 6, d)

    tabs = _rope_tables(s_lat, n_ctx)
    row = lambda v: v.reshape(1, -1)
    xa = jnp.concatenate([x, ctx], axis=1)
    out = None
    for l in range(DEPTH):
        need_ctx = l < DEPTH - 1
        nj = t // TQ if need_ctx else s_lat // TQ
        lambda_init = 0.8 - 0.6 * math.exp(-0.3 * l)
        win, wuq, wq2, wuv, wout = _layer_weights(w_in[l], mla_w_uq[l], mla_w_ukv[l], w_out[l])
        qn, kn, vn, qd, kd, vd, qm, kvm = _pre(
            xa, mod[l], row(norm_attn_g[l]), win, row(mla_q_norm_g[l]), wuq, row(mla_kv_norm_g[l]), wq2, tabs, s_lat)
        gsub = row(jnp.tile(diff_subln_g[l], DIFF_HEADS) * (1.0 - lambda_init))
        na, df, ml = _attention(qn, kn, vn, _na_bias(na_rel_bias[l], s_lat), qd, kd, vd, diff_lambda[l], gsub,
                                qm, kvm, s_lat, n_ctx, nj, lambda_init)
        rb = jnp.broadcast_to(router_bias[l][:, None], (N_EXPERTS, TQ))
        x1, h2, gates = _post(xa, na, df, ml, mod[l], wuv, wout, row(norm_ffn_g[l]), router_w[l], router_w[l].T, rb,
                              s_lat, nj)
        wgu = jnp.concatenate([expert_w_gate[l], expert_w_up[l]], axis=-1).astype(BF16)
        sgu = jnp.concatenate([shared_w_gate[l], shared_w_up[l]], axis=-1).astype(BF16)
        res = _moe(h2, gates, x1, mod[l], wgu, expert_w_down[l].astype(BF16), sgu, shared_w_down[l].astype(BF16),
                   row(final_norm_g), s_lat, t if need_ctx else s_lat, final=not need_ctx)
        if need_ctx:
            xa = res
        else:
            out = res
    return out
```

```python
import functools
import math

import numpy as np
import jax
import jax.numpy as jnp
from jax import lax
from jax.experimental import pallas as pl
from jax.experimental.pallas import tpu as pltpu

F32 = jnp.float32
BF16 = jnp.bfloat16

D_MODEL = 1024
DEPTH = 2
GRID_W = 64
EPS = 1e-6
NA_HEADS, NA_DIM, WIN_R, WIN_C = 4, 64, 8, 16
DIFF_HEADS, DIFF_QK_DIM, DIFF_V_DIM = 4, 32, 64
MLA_HEADS, MLA_NOPE, MLA_ROPE, MLA_V = 8, 64, 32, 64
MLA_Q_RANK, MLA_KV_RANK = 256, 128
ROPE_DIM = 32
ROPE_THETA = 10000.0
N_EXPERTS, TOP_K, N_GROUPS, TOPK_GROUPS = 64, 8, 8, 4
EXPERT_FF = 256
ROUTED_SCALE = 2.5

NA_W = NA_HEADS * NA_DIM
DIFF_W = DIFF_HEADS * DIFF_V_DIM
MLA_W = MLA_HEADS * MLA_V
O_DIFF = 3 * NA_W
O_MLA = O_DIFF + 3 * DIFF_W
IN_COLS = O_MLA + MLA_Q_RANK + MLA_KV_RANK + MLA_ROPE
HALF = ROPE_DIM // 2

LANES = 128
TQ = 256
Q_ROWS = TQ // GRID_W
NA_WIN_ROWS = Q_ROWS + WIN_R
NA_WIN = NA_WIN_ROWS * GRID_W
SLOT = 2 * LANES
W_IN_COLS = 3 * NA_W + 3 * DIFF_W + MLA_Q_RANK + 3 * LANES
EXPERTS_PER_STEP = 4
TQ_ATT = 512
NEG = -1e30
LOG2E = math.log2(math.e)
VMEM_LIMIT = 56 * 1024 * 1024


def _dot(a, b):
    return jnp.dot(a, b, preferred_element_type=F32)


def _dot_nt(a, b):
    return lax.dot_general(a, b, (((1,), (1,)), ((), ())), preferred_element_type=F32)


def _split(a):
    hi = a.astype(BF16)
    lo = (a - hi.astype(F32)).astype(BF16)
    return hi, lo


def _rms(x):
    return x * lax.rsqrt(jnp.mean(x * x, axis=-1, keepdims=True) + EPS)


def _silu(x):
    return x * jax.nn.sigmoid(x)


def _params(sem):
    return pltpu.CompilerParams(dimension_semantics=sem, vmem_limit_bytes=VMEM_LIMIT)


def _ada_kernel(c_ref, w_ref, b_ref, o_ref):
    a_hi, a_lo = _split(_silu(c_ref[...]))
    w_hi, w_lo = _split(w_ref[0])
    o_ref[0] = _dot(a_hi, w_hi) + _dot(a_hi, w_lo) + _dot(a_lo, w_hi) + b_ref[0]


def _ada_mod(c_all, w_ada, b_ada):
    depth, d, n = w_ada.shape
    rows = c_all.shape[0]
    nb = 1536
    return pl.pallas_call(
        _ada_kernel,
        grid=(depth, n // nb),
        in_specs=[pl.BlockSpec((rows, d), lambda l, j: (0, 0)),
                  pl.BlockSpec((1, d, nb), lambda l, j: (l, 0, j)),
                  pl.BlockSpec((1, 1, nb), lambda l, j: (l, 0, j))],
        out_specs=pl.BlockSpec((1, rows, nb), lambda l, j: (l, 0, j)),
        out_shape=jax.ShapeDtypeStruct((depth, rows, n), F32),
        compiler_params=_params(("arbitrary", "arbitrary")),
    )(c_all, w_ada, b_ada.reshape(depth, 1, n))


def _pre_kernel(x_ref, mod_ref, g_ref, win_ref, qg_ref, wuq_ref, kvg_ref, wq2_ref,
                cosd_ref, sind_ref, cosk_ref, sink_ref,
                qn_ref, kn_ref, vn_ref, qd_ref, kd_ref, vd_ref, qm_ref, kvm_ref):
    h = _rms(x_ref[0]) * g_ref[...]
    h = h * (1.0 + mod_ref[0, 1:2, :]) + mod_ref[0, 0:1, :]
    p = _dot(h.astype(BF16), win_ref[...])
    cd, sd = cosd_ref[...], sind_ref[...]

    qn_ref[0] = (p[:, 0:256] * (NA_DIM ** -0.5 * LOG2E)).astype(BF16)
    kn_ref[0] = p[:, 256:512].astype(BF16)
    vn_ref[0] = p[:, 512:768].astype(BF16)

    qa, qb = p[:, 768:896], p[:, 896:1024]
    sc = DIFF_QK_DIM ** -0.5 * LOG2E
    qd_ref[0, :, 0:LANES] = ((qa * cd - qb * sd) * sc).astype(BF16)
    qd_ref[0, :, LANES:2 * LANES] = ((qa * sd + qb * cd) * sc).astype(BF16)
    ka, kb = p[:, 1024:1152], p[:, 1152:1280]
    kd_ref[0, :, 0:LANES] = (ka * cd - kb * sd).astype(BF16)
    kd_ref[0, :, LANES:2 * LANES] = (ka * sd + kb * cd).astype(BF16)
    vd_ref[0] = p[:, 1280:1536].astype(BF16)

    cq = (_rms(p[:, 1536:1792]) * qg_ref[...]).astype(BF16)
    q1 = _dot(cq, wuq_ref[...])
    ms = (MLA_NOPE + MLA_ROPE) ** -0.5 * LOG2E
    ra, rb = q1[:, 512:640], q1[:, 640:768]
    q2 = jnp.concatenate([(q1[:, 0:512] * ms).astype(BF16),
                          ((ra * cd - rb * sd) * ms).astype(BF16),
                          ((ra * sd + rb * cd) * ms).astype(BF16)], axis=-1)
    qm_ref[0] = _dot(q2, wq2_ref[...]).astype(BF16)

    kvm_ref[0, :, 0:LANES] = (_rms(p[:, 1792:1920]) * kvg_ref[...]).astype(BF16)
    kvm_ref[0, :, LANES:2 * LANES] = (p[:, 1920:2048] * cosk_ref[...] + p[:, 2048:2176] * sink_ref[...]).astype(BF16)


def _pre(x_all, mod, g, win, qg, wuq, kvg, wq2, tabs, s_lat):
    b, t, d = x_all.shape
    nj = t // TQ
    njl = s_lat // TQ
    tok = lambda w: pl.BlockSpec((1, TQ, w), lambda i, j: (i, j, 0))
    full = lambda a: pl.BlockSpec(a.shape, lambda i, j: (0,) * a.ndim)
    tab = pl.BlockSpec((TQ, LANES), lambda i, j: (j, 0))
    widths = [NA_W, NA_W, NA_W, 2 * LANES, 2 * LANES, DIFF_W, MLA_HEADS * SLOT, SLOT]
    return pl.pallas_call(
        _pre_kernel,
        grid=(b, nj),
        in_specs=[tok(d),
                  pl.BlockSpec((1, 6, d), lambda i, j: (jnp.where(j < njl, i, b), 0, 0)),
                  full(g), full(win), full(qg), full(wuq), full(kvg), full(wq2), tab, tab, tab, tab],
        out_specs=[tok(w) for w in widths],
        out_shape=[jax.ShapeDtypeStruct((b, t, w), BF16) for w in widths],
        compiler_params=_params(("arbitrary", "arbitrary")),
    )(x_all, mod, g, win, qg, wuq, kvg, wq2, *tabs)


def _lane_mask(shape, lo, width):
    lane = lax.broadcasted_iota(jnp.int32, shape, len(shape) - 1)
    return (lane >= lo) & (lane < lo + width)


def _na_kernel(q_ref, k_ref, v_ref, bias_ref, o_ref, *, s_lat, n_ctx):
    j = pl.program_id(1)
    njl = s_lat // TQ
    rows = s_lat // GRID_W
    q = q_ref[0]
    kc = k_ref[0, s_lat:s_lat + n_ctx, :]
    vc = v_ref[0, s_lat:s_lat + n_ctx, :]

    def heads(latent):
        if latent:
            w0 = jnp.clip(j * Q_ROWS - WIN_R // 2, 0, rows - NA_WIN_ROWS)
            start = pl.multiple_of(w0 * GRID_W, GRID_W)
            kw = k_ref[0, pl.ds(start, NA_WIN), :]
            vw = v_ref[0, pl.ds(start, NA_WIN), :]
        acc = jnp.zeros((TQ, NA_W), F32)
        for h in range(NA_HEADS):
            mh = _lane_mask((TQ, NA_W), h * NA_DIM, NA_DIM)
            qh = jnp.where(mh, q, jnp.zeros_like(q))
            s2 = _dot_nt(qh, kc)
            m = jnp.max(s2, axis=-1, keepdims=True)
            if latent:
                s1 = _dot_nt(qh, kw) + bias_ref[0, h]
                m = jnp.maximum(m, jnp.max(s1, axis=-1, keepdims=True))
                e1 = jnp.exp2(s1 - m)
            e2 = jnp.exp2(s2 - m)
            l = jnp.sum(e2, axis=-1, keepdims=True)
            o = _dot(e2.astype(BF16), vc)
            if latent:
                l = l + jnp.sum(e1, axis=-1, keepdims=True)
                o = o + _dot(e1.astype(BF16), vw)
            acc = acc + jnp.where(mh, o * (1.0 / l), 0.0)
        o_ref[0] = acc.astype(BF16)

    @pl.when(j < njl)
    def _():
        heads(True)

    @pl.when(j >= njl)
    def _():
        heads(False)


def _diff_kernel(q_ref, k_ref, v_ref, lam_ref, gsub_ref, o_ref, *, s_lat, n_ctx, lambda_init, tq):
    j = pl.program_id(1)
    njl = s_lat // tq
    lv = lam_ref[...]
    lam = (jnp.exp(jnp.sum(lv[0:1] * lv[1:2], axis=-1, keepdims=True))
           - jnp.exp(jnp.sum(lv[2:3] * lv[3:4], axis=-1, keepdims=True)) + lambda_init)

    def heads(k, v, rows):
        q = q_ref[0, 0:rows, :]
        acc = jnp.zeros((rows, DIFF_W), F32)
        for h in range(DIFF_HEADS):
            es, ls = [], []
            for m in range(2):
                lo = (2 * h + m) * HALF
                mm = _lane_mask((rows, 2 * LANES), lo, HALF) | _lane_mask((rows, 2 * LANES), LANES + lo, HALF)
                s = _dot_nt(jnp.where(mm, q, jnp.zeros_like(q)), k)
                e = jnp.exp2(s - jnp.max(s, axis=-1, keepdims=True))
                es.append(e)
                ls.append(jnp.sum(e, axis=-1, keepdims=True))
            a = es[0] * (1.0 / ls[0]) - es[1] * (lam / ls[1])
            o = _dot(a.astype(BF16), v)
            om = jnp.where(_lane_mask((rows, DIFF_W), h * DIFF_V_DIM, DIFF_V_DIM), o, 0.0)
            ms = jnp.sum(om * om, axis=-1, keepdims=True) * (1.0 / DIFF_V_DIM)
            acc = acc + om * lax.rsqrt(ms + EPS)
        o_ref[0, 0:rows, :] = (acc * gsub_ref[...]).astype(BF16)

    @pl.when(j < njl)
    def _():
        heads(k_ref[0], v_ref[0], tq)

    @pl.when(j >= njl)
    def _():
        heads(k_ref[0, s_lat:s_lat + n_ctx, :], v_ref[0, s_lat:s_lat + n_ctx, :], min(tq, n_ctx))


def _mla_kernel(q_ref, kv_ref, o_ref, *, s_lat, n_ctx, tq):
    j = pl.program_id(1)
    njl = s_lat // tq

    def heads(kv, rows):
        v = kv[:, 0:LANES]
        for h in range(MLA_HEADS):
            s = _dot_nt(q_ref[0, 0:rows, h * SLOT:(h + 1) * SLOT], kv)
            e = jnp.exp2(s - jnp.max(s, axis=-1, keepdims=True))
            l = jnp.sum(e, axis=-1, keepdims=True)
            o_ref[0, 0:rows, h * LANES:(h + 1) * LANES] = (_dot(e.astype(BF16), v) * (1.0 / l)).astype(BF16)

    @pl.when(j < njl)
    def _():
        heads(kv_ref[0], tq)

    @pl.when(j >= njl)
    def _():
        heads(kv_ref[0, s_lat:s_lat + n_ctx, :], min(tq, n_ctx))


def _attention(qn, kn, vn, bias, qd, kd, vd, lam, gsub, qm, kvm, s_lat, n_ctx, nj, lambda_init):
    b, t, _ = qn.shape
    njl = s_lat // TQ
    blk = lambda w: pl.BlockSpec((1, TQ, w), lambda i, j: (i, j, 0))
    per_b = lambda w: pl.BlockSpec((1, t, w), lambda i, j: (i, 0, 0))
    full = lambda a: pl.BlockSpec(a.shape, lambda i, j: (0,) * a.ndim)
    sem = _params(("arbitrary", "arbitrary"))
    variant = lambda i, j: (jnp.where(j == 0, 0, jnp.where(j >= njl - 1, 2, 1)), 0, 0, 0)
    na = pl.pallas_call(
        functools.partial(_na_kernel, s_lat=s_lat, n_ctx=n_ctx),
        grid=(b, nj),
        in_specs=[blk(NA_W), per_b(NA_W), per_b(NA_W),
                  pl.BlockSpec((1, NA_HEADS, TQ, NA_WIN), variant)],
        out_specs=blk(NA_W),
        out_shape=jax.ShapeDtypeStruct((b, nj * TQ, NA_W), BF16),
        compiler_params=sem,
    )(qn, kn, vn, bias)
    sub = lambda w: pl.BlockSpec((1, TQ_ATT, w), lambda i, j: (i, j, 0))
    nja = pl.cdiv(nj * TQ, TQ_ATT)
    df = pl.pallas_call(
        functools.partial(_diff_kernel, s_lat=s_lat, n_ctx=n_ctx, lambda_init=lambda_init, tq=TQ_ATT),
        grid=(b, nja),
        in_specs=[sub(2 * LANES), per_b(2 * LANES), per_b(DIFF_W), full(lam), full(gsub)],
        out_specs=sub(DIFF_W),
        out_shape=jax.ShapeDtypeStruct((b, nj * TQ, DIFF_W), BF16),
        compiler_params=sem,
    )(qd, kd, vd, lam, gsub)
    ml = pl.pallas_call(
        functools.partial(_mla_kernel, s_lat=s_lat, n_ctx=n_ctx, tq=TQ_ATT),
        grid=(b, nja),
        in_specs=[sub(MLA_HEADS * SLOT), per_b(SLOT)],
        out_specs=sub(MLA_HEADS * LANES),
        out_shape=jax.ShapeDtypeStruct((b, nj * TQ, MLA_HEADS * LANES), BF16),
        compiler_params=sem,
    )(qm, kvm)
    return na, df, ml


def _route(sel_t, n_tok):
    gsz = N_EXPERTS // N_GROUPS
    sub = lax.broadcasted_iota(jnp.int32, (gsz, n_tok), 0)
    grp = []
    for g in range(N_GROUPS):
        v = sel_t[g * gsz:(g + 1) * gsz, :]
        m1 = jnp.max(v, axis=0, keepdims=True)
        first = jnp.min(jnp.where(v == m1, sub, gsz), axis=0, keepdims=True)
        m2 = jnp.max(jnp.where(sub == first, -jnp.inf, v), axis=0, keepdims=True)
        grp.append(m1 + m2)
    rows = []
    for g in range(N_GROUPS):
        rank = jnp.zeros((1, n_tok), jnp.int32)
        for o in range(N_GROUPS):
            if o == g:
                continue
            ahead = (grp[o] > grp[g]) | ((grp[o] == grp[g]) & (o < g))
            rank = rank + ahead.astype(jnp.int32)
        keep = rank < TOPK_GROUPS
        rows.append(jnp.where(keep, sel_t[g * gsz:(g + 1) * gsz, :], -jnp.inf))
    v = jnp.concatenate(rows, axis=0)
    eio = lax.broadcasted_iota(jnp.int32, (N_EXPERTS, n_tok), 0)
    chosen = jnp.zeros((N_EXPERTS, n_tok), F32)
    for _ in range(TOP_K):
        m = jnp.max(v, axis=0, keepdims=True)
        idx = jnp.min(jnp.where(v == m, eio, N_EXPERTS), axis=0, keepdims=True)
        hit = eio == idx
        chosen = jnp.where(hit, 1.0, chosen)
        v = jnp.where(hit, -jnp.inf, v)
    return chosen


def _post_kernel(x_ref, na_ref, df_ref, ml_ref, mod_ref, wuv_ref, wout_ref, g_ref, rw_ref, rwt_ref, rb_ref,
                 x1_ref, h2_ref, gate_ref):
    mparts = [_dot(ml_ref[0, :, p * SLOT:(p + 1) * SLOT], wuv_ref[p]).astype(BF16) for p in range(MLA_HEADS // 2)]
    a = (_dot(na_ref[0], wout_ref[0:NA_W, :]) + _dot(df_ref[0], wout_ref[NA_W:NA_W + DIFF_W, :])
         + _dot(jnp.concatenate(mparts, axis=-1), wout_ref[NA_W + DIFF_W:, :]))
    x1 = x_ref[0] + mod_ref[0, 2:3, :] * a
    x1_ref[0] = x1
    h2 = _rms(x1) * g_ref[...]
    h2 = h2 * (1.0 + mod_ref[0, 4:5, :]) + mod_ref[0, 3:4, :]
    h2_ref[0] = h2.astype(BF16)

    h_hi, h_lo = _split(h2)
    w_hi, w_lo = _split(rw_ref[...])
    wt_hi, wt_lo = _split(rwt_ref[...])
    scores = jax.nn.sigmoid(_dot(h_hi, w_hi) + _dot(h_hi, w_lo) + _dot(h_lo, w_hi))
    sel_t = jax.nn.sigmoid(_dot_nt(wt_hi, h_hi) + _dot_nt(wt_lo, h_hi) + _dot_nt(wt_hi, h_lo)) + rb_ref[...]
    chosen_t = _route(sel_t, TQ).astype(BF16)
    eye = (lax.broadcasted_iota(jnp.int32, (TQ, TQ), 0) == lax.broadcasted_iota(jnp.int32, (TQ, TQ), 1))
    chosen = _dot_nt(eye.astype(F32).astype(BF16), chosen_t)
    w = scores * chosen
    gate_ref[0] = w * (ROUTED_SCALE / jnp.sum(w, axis=-1, keepdims=True))


def _post(x_all, na, df, ml, mod, wuv, wout, g, rw, rwt, rb, s_lat, nj):
    b, t, d = x_all.shape
    njl = s_lat // TQ
    blk = lambda w: pl.BlockSpec((1, TQ, w), lambda i, j: (i, j, 0))
    full = lambda a: pl.BlockSpec(a.shape, lambda i, j: (0,) * a.ndim)
    return pl.pallas_call(
        _post_kernel,
        grid=(b, nj),
        in_specs=[blk(d), blk(NA_W), blk(DIFF_W), blk(MLA_HEADS * LANES),
                  pl.BlockSpec((1, 6, d), lambda i, j: (jnp.where(j < njl, i, b), 0, 0)),
                  full(wuv), full(wout), full(g), full(rw), full(rwt), full(rb)],
        out_specs=[blk(d), blk(d), blk(N_EXPERTS)],
        out_shape=[jax.ShapeDtypeStruct((b, nj * TQ, d), F32), jax.ShapeDtypeStruct((b, nj * TQ, d), BF16),
                   jax.ShapeDtypeStruct((b, nj * TQ, N_EXPERTS), F32)],
        compiler_params=_params(("arbitrary", "arbitrary")),
    )(x_all, na, df, ml, mod, wuv, wout, g, rw, rwt, rb)


def _moe_kernel(h_ref, gate_ref, x_ref, modl_ref, modc_ref, wgu_ref, wd_ref, sgu_ref, sd_ref, fg_ref,
                o_ref, acc_ref, act_ref, *, s_lat, tm, final):
    i2 = pl.program_id(1)
    e0 = pl.program_id(2)
    h = h_ref[0]
    ff = EXPERT_FF

    @pl.when(e0 == 0)
    def _():
        u = _dot(h, sgu_ref[...])
        acc_ref[...] = _dot((_silu(u[:, 0:ff]) * u[:, ff:2 * ff]).astype(BF16), sd_ref[...])

    gates = gate_ref[0]
    lane = lax.broadcasted_iota(jnp.int32, gates.shape, 1)
    for k in range(EXPERTS_PER_STEP):
        u = _dot(h, wgu_ref[k])
        gcol = jnp.sum(jnp.where(lane == e0 * EXPERTS_PER_STEP + k, gates, 0.0), axis=-1, keepdims=True)
        act_ref[:, k * ff:(k + 1) * ff] = (_silu(u[:, 0:ff]) * u[:, ff:2 * ff] * gcol).astype(BF16)
    acc_ref[...] += _dot(act_ref[...], wd_ref[0])

    @pl.when(e0 == pl.num_programs(2) - 1)
    def _():
        row = i2 * tm + lax.broadcasted_iota(jnp.int32, (tm, 1), 0)
        g2 = jnp.where(row < s_lat, modl_ref[0, 5:6, :], modc_ref[0, 5:6, :])
        y = x_ref[0] + g2 * acc_ref[...]
        if final:
            y = _rms(y) * fg_ref[...]
        o_ref[0] = y


def _moe(h2, gates, x1, mod, wgu, wd, sgu, sd, fg, s_lat, rows, final):
    b, t, d = x1.shape
    tm = rows // 2
    ne = wgu.shape[0] // EXPERTS_PER_STEP
    blk = lambda w: pl.BlockSpec((1, tm, w), lambda i, j, e: (i, j, 0))
    full = lambda a: pl.BlockSpec(a.shape, lambda i, j, e: (0,) * a.ndim)
    return pl.pallas_call(
        functools.partial(_moe_kernel, s_lat=s_lat, tm=tm, final=final),
        grid=(b, 2, ne),
        in_specs=[blk(d), blk(N_EXPERTS), blk(d),
                  pl.BlockSpec((1, 6, d), lambda i, j, e: (i, 0, 0)),
                  pl.BlockSpec((1, 6, d), lambda i, j, e: (b, 0, 0)),
                  pl.BlockSpec((EXPERTS_PER_STEP, d, 2 * EXPERT_FF), lambda i, j, e: (e, 0, 0)),
                  pl.BlockSpec((1, EXPERTS_PER_STEP * EXPERT_FF, d), lambda i, j, e: (e, 0, 0)),
                  full(sgu), full(sd), full(fg)],
        out_specs=blk(d),
        out_shape=jax.ShapeDtypeStruct((b, rows, d), F32),
        scratch_shapes=[pltpu.VMEM((tm, d), F32), pltpu.VMEM((tm, EXPERTS_PER_STEP * EXPERT_FF), BF16)],
        compiler_params=_params(("arbitrary", "arbitrary", "arbitrary")),
    )(h2, gates, x1, mod, mod, wgu, wd.reshape(ne, EXPERTS_PER_STEP * EXPERT_FF, d), sgu, sd, fg)


def _w_in_columns():
    z = IN_COLS
    cols = list(range(0, 3 * NA_W))
    for base in (O_DIFF, O_DIFF + DIFF_W):
        for half in range(2):
            for m in range(2 * DIFF_HEADS):
                cols += [base + m * DIFF_QK_DIM + half * HALF + i for i in range(HALF)]
    cols += list(range(O_DIFF + 2 * DIFF_W, O_MLA + MLA_Q_RANK + MLA_KV_RANK))
    kr = O_MLA + MLA_Q_RANK + MLA_KV_RANK
    cols += list(range(kr, kr + MLA_ROPE)) + [z] * (LANES - MLA_ROPE)
    rot = list(range(kr + HALF, kr + MLA_ROPE)) + list(range(kr, kr + HALF)) + [z] * (LANES - MLA_ROPE)
    sign = np.ones((W_IN_COLS,), np.float32)
    sign[len(cols):len(cols) + HALF] = -1.0
    cols += rot
    assert len(cols) == W_IN_COLS
    return np.asarray(cols, np.int32), sign


def _w_uq_columns():
    per = MLA_NOPE + MLA_ROPE
    cols = [h * per + i for h in range(MLA_HEADS) for i in range(MLA_NOPE)]
    for half in range(2):
        cols += [h * per + MLA_NOPE + half * HALF + i for h in range(MLA_HEADS) for i in range(HALF)]
    return np.asarray(cols, np.int32)


def _layer_weights(w_in, w_uq, w_ukv, w_out):
    cols, sign = _w_in_columns()
    win = (jnp.concatenate([w_in, jnp.zeros((w_in.shape[0], 1), F32)], axis=1)[:, cols] * sign).astype(BF16)
    wuq = w_uq[:, _w_uq_columns()].astype(BF16)
    kv = w_ukv.reshape(MLA_KV_RANK, MLA_HEADS, MLA_NOPE + MLA_V)
    wq2 = jnp.zeros((MLA_HEADS * MLA_NOPE + 2 * LANES, MLA_HEADS * SLOT), F32)
    place = np.zeros((2 * LANES, MLA_HEADS * SLOT), np.float32)
    for h in range(MLA_HEADS):
        wq2 = wq2.at[h * MLA_NOPE:(h + 1) * MLA_NOPE, h * SLOT:h * SLOT + MLA_KV_RANK].set(kv[:, h, :MLA_NOPE].T)
        for i in range(HALF):
            place[h * HALF + i, h * SLOT + MLA_KV_RANK + i] = 1.0
            place[LANES + h * HALF + i, h * SLOT + MLA_KV_RANK + HALF + i] = 1.0
    wq2 = wq2.at[MLA_HEADS * MLA_NOPE:, :].set(place).astype(BF16)
    wuv = jnp.zeros((MLA_HEADS // 2, SLOT, 2 * MLA_V), F32)
    for h in range(MLA_HEADS):
        wuv = wuv.at[h // 2, (h % 2) * LANES:(h % 2 + 1) * LANES, (h % 2) * MLA_V:(h % 2 + 1) * MLA_V].set(
            kv[:, h, MLA_NOPE:])
    return win, wuq, wq2, wuv.astype(BF16), w_out.astype(BF16)


def _rope_tables(s_lat, n_ctx):
    t = np.arange(s_lat)
    n_freq = ROPE_DIM // 4
    freq = ROPE_THETA ** (-jnp.arange(n_freq, dtype=F32) / n_freq)
    ang = jnp.concatenate([(t // GRID_W).astype(np.float32)[:, None] * freq,
                           (t % GRID_W).astype(np.float32)[:, None] * freq], axis=-1)
    cos, sin = jnp.cos(ang), jnp.sin(ang)
    one, zero = jnp.ones((n_ctx, LANES), F32), jnp.zeros((n_ctx, LANES), F32)
    rep = LANES // HALF
    cosd = jnp.concatenate([jnp.tile(cos, (1, rep)), one], axis=0)
    sind = jnp.concatenate([jnp.tile(sin, (1, rep)), zero], axis=0)
    return cosd, sind, cosd, sind


def _na_bias(rel_bias, s_lat):
    qc = np.arange(GRID_W)
    cs = np.clip(qc - WIN_C // 2, 0, GRID_W - WIN_C)
    col_ok = (qc[None, :] >= cs[:, None]) & (qc[None, :] < cs[:, None] + WIN_C)
    dc = qc[None, :] - qc[:, None] + WIN_C - 1
    pick = (dc[:, :, None] == np.arange(2 * WIN_C - 1)[None, None, :]) & col_ok[:, :, None]
    tiles = jnp.einsum('hij,cdj->hicd', rel_bias, jnp.asarray(pick, F32), precision=lax.Precision.HIGHEST)
    tiles = jnp.where(col_ok[None, None], tiles * LOG2E, NEG)
    masked = jnp.full((NA_HEADS, GRID_W, GRID_W), NEG, F32)
    out = []
    for delta, lo in ((0, 0), (-(WIN_R // 2), None), (-WIN_R, WIN_R // 2)):
        rows = []
        for qr in range(Q_ROWS):
            row = []
            for kr in range(NA_WIN_ROWS):
                rel = delta + kr - qr
                ok = (-(WIN_R // 2) <= rel < WIN_R // 2) if lo is None else (lo <= kr < lo + WIN_R)
                row.append(tiles[:, rel + WIN_R - 1] if ok else masked)
            rows.append(jnp.concatenate(row, axis=-1))
        out.append(jnp.concatenate(rows, axis=-2))
    return jnp.stack(out, axis=0)


def kernel(x, c, ctx, c_ctx, w_ada, b_ada, norm_attn_g, norm_ffn_g, w_in, na_rel_bias, diff_lambda, diff_subln_g, mla_q_norm_g, mla_w_uq, mla_kv_norm_g, mla_w_ukv, w_out, router_w, router_bias, expert_w_gate, expert_w_up, expert_w_down, shared_w_gate, shared_w_up, shared_w_down, final_norm_g):
    b, s_lat, d = x.shape
    n_ctx = ctx.shape[1]
    assert d == D_MODEL and n_ctx == TQ and s_lat % TQ == 0 and s_lat // GRID_W >= NA_WIN_ROWS + Q_ROWS
    t = s_lat + n_ctx

    c_rows = 32
    c_all = jnp.concatenate([c, c_ctx[None, :], jnp.zeros((c_rows - b - 1, d), F32)], axis=0)
    mod = _ada_mod(c_all, w_ada, b_ada)[:, :b + 1].reshape(DEPTH, b + 1, 6, d)

    tabs = _rope_tables(s_lat, n_ctx)
    row = lambda v: v.reshape(1, -1)
    xa = jnp.concatenate([x, ctx], axis=1)
    out = None
    for l in range(DEPTH):
        need_ctx = l < DEPTH - 1
        nj = t // TQ if need_ctx else s_lat // TQ
        lambda_init = 0.8 - 0.6 * math.exp(-0.3 * l)
        win, wuq, wq2, wuv, wout = _layer_weights(w_in[l], mla_w_uq[l], mla_w_ukv[l], w_out[l])
        qn, kn, vn, qd, kd, vd, qm, kvm = _pre(
            xa, mod[l], row(norm_attn_g[l]), win, row(mla_q_norm_g[l]), wuq, row(mla_kv_norm_g[l]), wq2, tabs, s_lat)
        gsub = row(jnp.tile(diff_subln_g[l], DIFF_HEADS) * (1.0 - lambda_init))
        na, df, ml = _attention(qn, kn, vn, _na_bias(na_rel_bias[l], s_lat), qd, kd, vd, diff_lambda[l], gsub,
                                qm, kvm, s_lat, n_ctx, nj, lambda_init)
        rb = jnp.broadcast_to(router_bias[l][:, None], (N_EXPERTS, TQ))
        x1, h2, gates = _post(xa, na, df, ml, mod[l], wuv, wout, row(norm_ffn_g[l]), router_w[l], router_w[l].T, rb,
                              s_lat, nj)
        wgu = jnp.concatenate([expert_w_gate[l], expert_w_up[l]], axis=-1).astype(BF16)
        sgu = jnp.concatenate([shared_w_gate[l], shared_w_up[l]], axis=-1).astype(BF16)
        res = _moe(h2, gates, x1, mod[l], wgu, expert_w_down[l].astype(BF16), sgu, shared_w_down[l].astype(BF16),
                   row(final_norm_g), s_lat, t if need_ctx else s_lat, final=not need_ctx)
        if need_ctx:
            xa = res
        else:
            out = res
    return out
```

```python
import functools
import math

import numpy as np
import jax
import jax.numpy as jnp
from jax import lax
from jax.experimental import pallas as pl
from jax.experimental.pallas import tpu as pltpu

F32 = jnp.float32
BF16 = jnp.bfloat16

D_MODEL = 1024
DEPTH = 2
GRID_W = 64
EPS = 1e-6
NA_HEADS, NA_DIM, WIN_R, WIN_C = 4, 64, 8, 16
DIFF_HEADS, DIFF_QK_DIM, DIFF_V_DIM = 4, 32, 64
MLA_HEADS, MLA_NOPE, MLA_ROPE, MLA_V = 8, 64, 32, 64
MLA_Q_RANK, MLA_KV_RANK = 256, 128
ROPE_DIM = 32
ROPE_THETA = 10000.0
N_EXPERTS, TOP_K, N_GROUPS, TOPK_GROUPS = 64, 8, 8, 4
EXPERT_FF = 256
ROUTED_SCALE = 2.5

NA_W = NA_HEADS * NA_DIM
DIFF_W = DIFF_HEADS * DIFF_V_DIM
MLA_W = MLA_HEADS * MLA_V
O_DIFF = 3 * NA_W
O_MLA = O_DIFF + 3 * DIFF_W
IN_COLS = O_MLA + MLA_Q_RANK + MLA_KV_RANK + MLA_ROPE
HALF = ROPE_DIM // 2

LANES = 128
TQ = 256
Q_ROWS = TQ // GRID_W
NA_WIN_ROWS = Q_ROWS + WIN_R
NA_WIN = NA_WIN_ROWS * GRID_W
SLOT = 2 * LANES
W_IN_COLS = 3 * NA_W + 3 * DIFF_W + MLA_Q_RANK + 3 * LANES
EXPERTS_PER_STEP = 4
TQ_ATT = 512
NEG = -1e30
LOG2E = math.log2(math.e)
VMEM_LIMIT = 56 * 1024 * 1024


def _dot(a, b):
    return jnp.dot(a, b, preferred_element_type=F32)


def _dot_nt(a, b):
    return lax.dot_general(a, b, (((1,), (1,)), ((), ())), preferred_element_type=F32)


def _split(a):
    hi = a.astype(BF16)
    lo = (a - hi.astype(F32)).astype(BF16)
    return hi, lo


def _rms(x):
    return x * lax.rsqrt(jnp.mean(x * x, axis=-1, keepdims=True) + EPS)


def _silu(x):
    return x * jax.nn.sigmoid(x)


def _params(sem):
    return pltpu.CompilerParams(dimension_semantics=sem, vmem_limit_bytes=VMEM_LIMIT)


def _ada_kernel(c_ref, w_ref, b_ref, o_ref):
    a_hi, a_lo = _split(_silu(c_ref[...]))
    w_hi, w_lo = _split(w_ref[0])
    o_ref[0] = _dot(a_hi, w_hi) + _dot(a_hi, w_lo) + _dot(a_lo, w_hi) + b_ref[0]


def _ada_mod(c_all, w_ada, b_ada):
    depth, d, n = w_ada.shape
    rows = c_all.shape[0]
    nb = 1536
    return pl.pallas_call(
        _ada_kernel,
        grid=(depth, n // nb),
        in_specs=[pl.BlockSpec((rows, d), lambda l, j: (0, 0)),
                  pl.BlockSpec((1, d, nb), lambda l, j: (l, 0, j)),
                  pl.BlockSpec((1, 1, nb), lambda l, j: (l, 0, j))],
        out_specs=pl.BlockSpec((1, rows, nb), lambda l, j: (l, 0, j)),
        out_shape=jax.ShapeDtypeStruct((depth, rows, n), F32),
        compiler_params=_params(("arbitrary", "arbitrary")),
    )(c_all, w_ada, b_ada.reshape(depth, 1, n))


def _pre_kernel(x_ref, mod_ref, g_ref, win_ref, qg_ref, wuq_ref, kvg_ref, wq2_ref,
                cosd_ref, sind_ref, cosk_ref, sink_ref,
                qn_ref, kn_ref, vn_ref, qd_ref, kd_ref, vd_ref, qm_ref, kvm_ref):
    h = _rms(x_ref[0]) * g_ref[...]
    h = h * (1.0 + mod_ref[0, 1:2, :]) + mod_ref[0, 0:1, :]
    p = _dot(h.astype(BF16), win_ref[...])
    cd, sd = cosd_ref[...], sind_ref[...]

    qn_ref[0] = (p[:, 0:256] * (NA_DIM ** -0.5 * LOG2E)).astype(BF16)
    kn_ref[0] = p[:, 256:512].astype(BF16)
    vn_ref[0] = p[:, 512:768].astype(BF16)

    qa, qb = p[:, 768:896], p[:, 896:1024]
    sc = DIFF_QK_DIM ** -0.5 * LOG2E
    qd_ref[0, :, 0:LANES] = ((qa * cd - qb * sd) * sc).astype(BF16)
    qd_ref[0, :, LANES:2 * LANES] = ((qa * sd + qb * cd) * sc).astype(BF16)
    ka, kb = p[:, 1024:1152], p[:, 1152:1280]
    kd_ref[0, :, 0:LANES] = (ka * cd - kb * sd).astype(BF16)
    kd_ref[0, :, LANES:2 * LANES] = (ka * sd + kb * cd).astype(BF16)
    vd_ref[0] = p[:, 1280:1536].astype(BF16)

    cq = (_rms(p[:, 1536:1792]) * qg_ref[...]).astype(BF16)
    q1 = _dot(cq, wuq_ref[...])
    ms = (MLA_NOPE + MLA_ROPE) ** -0.5 * LOG2E
    ra, rb = q1[:, 512:640], q1[:, 640:768]
    q2 = jnp.concatenate([(q1[:, 0:512] * ms).astype(BF16),
                          ((ra * cd - rb * sd) * ms).astype(BF16),
                          ((ra * sd + rb * cd) * ms).astype(BF16)], axis=-1)
    qm_ref[0] = _dot(q2, wq2_ref[...]).astype(BF16)

    kvm_ref[0, :, 0:LANES] = (_rms(p[:, 1792:1920]) * kvg_ref[...]).astype(BF16)
    kvm_ref[0, :, LANES:2 * LANES] = (p[:, 1920:2048] * cosk_ref[...] + p[:, 2048:2176] * sink_ref[...]).astype(BF16)


def _pre(x_all, mod, g, win, qg, wuq, kvg, wq2, tabs, s_lat):
    b, t, d = x_all.shape
    nj = t // TQ
    njl = s_lat // TQ
    tok = lambda w: pl.BlockSpec((1, TQ, w), lambda i, j: (i, j, 0))
    full = lambda a: pl.BlockSpec(a.shape, lambda i, j: (0,) * a.ndim)
    tab = pl.BlockSpec((TQ, LANES), lambda i, j: (j, 0))
    widths = [NA_W, NA_W, NA_W, 2 * LANES, 2 * LANES, DIFF_W, MLA_HEADS * SLOT, SLOT]
    return pl.pallas_call(
        _pre_kernel,
        grid=(b, nj),
        in_specs=[tok(d),
                  pl.BlockSpec((1, 6, d), lambda i, j: (jnp.where(j < njl, i, b), 0, 0)),
                  full(g), full(win), full(qg), full(wuq), full(kvg), full(wq2), tab, tab, tab, tab],
        out_specs=[tok(w) for w in widths],
        out_shape=[jax.ShapeDtypeStruct((b, t, w), BF16) for w in widths],
        compiler_params=_params(("arbitrary", "arbitrary")),
    )(x_all, mod, g, win, qg, wuq, kvg, wq2, *tabs)


def _lane_mask(shape, lo, width):
    lane = lax.broadcasted_iota(jnp.int32, shape, len(shape) - 1)
    return (lane >= lo) & (lane < lo + width)


def _na_kernel(q_ref, k_ref, v_ref, bias_ref, o_ref, *, s_lat, n_ctx):
    j = pl.program_id(1)
    njl = s_lat // TQ
    rows = s_lat // GRID_W
    q = q_ref[0]
    kc = k_ref[0, s_lat:s_lat + n_ctx, :]
    vc = v_ref[0, s_lat:s_lat + n_ctx, :]

    def heads(latent):
        if latent:
            w0 = jnp.clip(j * Q_ROWS - WIN_R // 2, 0, rows - NA_WIN_ROWS)
            start = pl.multiple_of(w0 * GRID_W, GRID_W)
            kw = k_ref[0, pl.ds(start, NA_WIN), :]
            vw = v_ref[0, pl.ds(start, NA_WIN), :]
        acc = jnp.zeros((TQ, NA_W), F32)
        for h in range(NA_HEADS):
            mh = _lane_mask((TQ, NA_W), h * NA_DIM, NA_DIM)
            qh = jnp.where(mh, q, jnp.zeros_like(q))
            s2 = _dot_nt(qh, kc)
            m = jnp.max(s2, axis=-1, keepdims=True)
            if latent:
                s1 = _dot_nt(qh, kw) + bias_ref[0, h]
                m = jnp.maximum(m, jnp.max(s1, axis=-1, keepdims=True))
                e1 = jnp.exp2(s1 - m)
            e2 = jnp.exp2(s2 - m)
            l = jnp.sum(e2, axis=-1, keepdims=True)
            o = _dot(e2.astype(BF16), vc)
            if latent:
                l = l + jnp.sum(e1, axis=-1, keepdims=True)
                o = o + _dot(e1.astype(BF16), vw)
            acc = acc + jnp.where(mh, o * (1.0 / l), 0.0)
        o_ref[0] = acc.astype(BF16)

    @pl.when(j < njl)
    def _():
        heads(True)

    @pl.when(j >= njl)
    def _():
        heads(False)


def _diff_kernel(q_ref, k_ref, v_ref, lam_ref, gsub_ref, o_ref, *, s_lat, n_ctx, lambda_init, tq):
    j = pl.program_id(1)
    njl = s_lat // tq
    lv = lam_ref[...]
    lam = (jnp.exp(jnp.sum(lv[0:1] * lv[1:2], axis=-1, keepdims=True))
           - jnp.exp(jnp.sum(lv[2:3] * lv[3:4], axis=-1, keepdims=True)) + lambda_init)

    def heads(k, v, rows):
        q = q_ref[0, 0:rows, :]
        acc = jnp.zeros((rows, DIFF_W), F32)

        def scores(h):
            out = []
            for m in range(2):
                lo = (2 * h + m) * HALF
                mm = _lane_mask((rows, 2 * LANES), lo, HALF) | _lane_mask((rows, 2 * LANES), LANES + lo, HALF)
                out.append(_dot_nt(jnp.where(mm, q, jnp.zeros_like(q)), k))
            return out

        nxt = scores(0)
        for h in range(DIFF_HEADS):
            cur = nxt
            if h + 1 < DIFF_HEADS:
                nxt = scores(h + 1)
            es, ls = [], []
            for m in range(2):
                s = cur[m]
                e = jnp.exp2(s - jnp.max(s, axis=-1, keepdims=True))
                es.append(e)
                ls.append(jnp.sum(e, axis=-1, keepdims=True))
            a = es[0] * (1.0 / ls[0]) - es[1] * (lam / ls[1])
            o = _dot(a.astype(BF16), v)
            om = jnp.where(_lane_mask((rows, DIFF_W), h * DIFF_V_DIM, DIFF_V_DIM), o, 0.0)
            ms = jnp.sum(om * om, axis=-1, keepdims=True) * (1.0 / DIFF_V_DIM)
            acc = acc + om * lax.rsqrt(ms + EPS)
        o_ref[0, 0:rows, :] = (acc * gsub_ref[...]).astype(BF16)

    @pl.when(j < njl)
    def _():
        heads(k_ref[0], v_ref[0], tq)

    @pl.when(j >= njl)
    def _():
        heads(k_ref[0, s_lat:s_lat + n_ctx, :], v_ref[0, s_lat:s_lat + n_ctx, :], min(tq, n_ctx))


def _mla_kernel(q_ref, kv_ref, o_ref, *, s_lat, n_ctx, tq):
    j = pl.program_id(1)
    njl = s_lat // tq

    def heads(kv, rows):
        v = kv[:, 0:LANES]
        for h in range(MLA_HEADS):
            s = _dot_nt(q_ref[0, 0:rows, h * SLOT:(h + 1) * SLOT], kv)
            e = jnp.exp2(s - jnp.max(s, axis=-1, keepdims=True))
            l = jnp.sum(e, axis=-1, keepdims=True)
            o_ref[0, 0:rows, h * LANES:(h + 1) * LANES] = (_dot(e.astype(BF16), v) * (1.0 / l)).astype(BF16)

    @pl.when(j < njl)
    def _():
        heads(kv_ref[0], tq)

    @pl.when(j >= njl)
    def _():
        heads(kv_ref[0, s_lat:s_lat + n_ctx, :], min(tq, n_ctx))


def _attention(qn, kn, vn, bias, qd, kd, vd, lam, gsub, qm, kvm, s_lat, n_ctx, nj, lambda_init):
    b, t, _ = qn.shape
    njl = s_lat // TQ
    blk = lambda w: pl.BlockSpec((1, TQ, w), lambda i, j: (i, j, 0))
    per_b = lambda w: pl.BlockSpec((1, t, w), lambda i, j: (i, 0, 0))
    full = lambda a: pl.BlockSpec(a.shape, lambda i, j: (0,) * a.ndim)
    sem = _params(("arbitrary", "arbitrary"))
    variant = lambda i, j: (jnp.where(j == 0, 0, jnp.where(j >= njl - 1, 2, 1)), 0, 0, 0)
    na = pl.pallas_call(
        functools.partial(_na_kernel, s_lat=s_lat, n_ctx=n_ctx),
        grid=(b, nj),
        in_specs=[blk(NA_W), per_b(NA_W), per_b(NA_W),
                  pl.BlockSpec((1, NA_HEADS, TQ, NA_WIN), variant)],
        out_specs=blk(NA_W),
        out_shape=jax.ShapeDtypeStruct((b, nj * TQ, NA_W), BF16),
        compiler_params=sem,
    )(qn, kn, vn, bias)
    sub = lambda w: pl.BlockSpec((1, TQ_ATT, w), lambda i, j: (i, j, 0))
    nja = pl.cdiv(nj * TQ, TQ_ATT)
    df = pl.pallas_call(
        functools.partial(_diff_kernel, s_lat=s_lat, n_ctx=n_ctx, lambda_init=lambda_init, tq=TQ_ATT),
        grid=(b, nja),
        in_specs=[sub(2 * LANES), per_b(2 * LANES), per_b(DIFF_W), full(lam), full(gsub)],
        out_specs=sub(DIFF_W),
        out_shape=jax.ShapeDtypeStruct((b, nj * TQ, DIFF_W), BF16),
        compiler_params=sem,
    )(qd, kd, vd, lam, gsub)
    ml = pl.pallas_call(
        functools.partial(_mla_kernel, s_lat=s_lat, n_ctx=n_ctx, tq=TQ_ATT),
        grid=(b, nja),
        in_specs=[sub(MLA_HEADS * SLOT), per_b(SLOT)],
        out_specs=sub(MLA_HEADS * LANES),
        out_shape=jax.ShapeDtypeStruct((b, nj * TQ, MLA_HEADS * LANES), BF16),
        compiler_params=sem,
    )(qm, kvm)
    return na, df, ml


def _route(sel_t, n_tok):
    gsz = N_EXPERTS // N_GROUPS
    sub = lax.broadcasted_iota(jnp.int32, (gsz, n_tok), 0)
    grp = []
    for g in range(N_GROUPS):
        v = sel_t[g * gsz:(g + 1) * gsz, :]
        m1 = jnp.max(v, axis=0, keepdims=True)
        first = jnp.min(jnp.where(v == m1, sub, gsz), axis=0, keepdims=True)
        m2 = jnp.max(jnp.where(sub == first, -jnp.inf, v), axis=0, keepdims=True)
        grp.append(m1 + m2)
    rows = []
    for g in range(N_GROUPS):
        rank = jnp.zeros((1, n_tok), jnp.int32)
        for o in range(N_GROUPS):
            if o == g:
                continue
            ahead = (grp[o] > grp[g]) | ((grp[o] == grp[g]) & (o < g))
            rank = rank + ahead.astype(jnp.int32)
        keep = rank < TOPK_GROUPS
        rows.append(jnp.where(keep, sel_t[g * gsz:(g + 1) * gsz, :], -jnp.inf))
    v = jnp.concatenate(rows, axis=0)
    eio = lax.broadcasted_iota(jnp.int32, (N_EXPERTS, n_tok), 0)
    chosen = jnp.zeros((N_EXPERTS, n_tok), F32)
    for _ in range(TOP_K):
        m = jnp.max(v, axis=0, keepdims=True)
        idx = jnp.min(jnp.where(v == m, eio, N_EXPERTS), axis=0, keepdims=True)
        hit = eio == idx
        chosen = jnp.where(hit, 1.0, chosen)
        v = jnp.where(hit, -jnp.inf, v)
    return chosen


def _post_kernel(x_ref, na_ref, df_ref, ml_ref, mod_ref, wuv_ref, wout_ref, g_ref, rw_ref, rwt_ref, rb_ref,
                 x1_ref, h2_ref, gate_ref):
    mparts = [_dot(ml_ref[0, :, p * SLOT:(p + 1) * SLOT], wuv_ref[p]).astype(BF16) for p in range(MLA_HEADS // 2)]
    a = (_dot(na_ref[0], wout_ref[0:NA_W, :]) + _dot(df_ref[0], wout_ref[NA_W:NA_W + DIFF_W, :])
         + _dot(jnp.concatenate(mparts, axis=-1), wout_ref[NA_W + DIFF_W:, :]))
    x1 = x_ref[0] + mod_ref[0, 2:3, :] * a
    x1_ref[0] = x1
    h2 = _rms(x1) * g_ref[...]
    h2 = h2 * (1.0 + mod_ref[0, 4:5, :]) + mod_ref[0, 3:4, :]
    h2_ref[0] = h2.astype(BF16)

    h_hi, h_lo = _split(h2)
    w_hi, w_lo = _split(rw_ref[...])
    wt_hi, wt_lo = _split(rwt_ref[...])
    scores = jax.nn.sigmoid(_dot(h_hi, w_hi) + _dot(h_hi, w_lo) + _dot(h_lo, w_hi))
    sel_t = jax.nn.sigmoid(_dot_nt(wt_hi, h_hi) + _dot_nt(wt_lo, h_hi) + _dot_nt(wt_hi, h_lo)) + rb_ref[...]
    chosen_t = _route(sel_t, TQ).astype(BF16)
    eye = (lax.broadcasted_iota(jnp.int32, (TQ, TQ), 0) == lax.broadcasted_iota(jnp.int32, (TQ, TQ), 1))
    chosen = _dot_nt(eye.astype(F32).astype(BF16), chosen_t)
    w = scores * chosen
    gate_ref[0] = w * (ROUTED_SCALE / jnp.sum(w, axis=-1, keepdims=True))


def _post(x_all, na, df, ml, mod, wuv, wout, g, rw, rwt, rb, s_lat, nj):
    b, t, d = x_all.shape
    njl = s_lat // TQ
    blk = lambda w: pl.BlockSpec((1, TQ, w), lambda i, j: (i, j, 0))
    full = lambda a: pl.BlockSpec(a.shape, lambda i, j: (0,) * a.ndim)
    return pl.pallas_call(
        _post_kernel,
        grid=(b, nj),
        in_specs=[blk(d), blk(NA_W), blk(DIFF_W), blk(MLA_HEADS * LANES),
                  pl.BlockSpec((1, 6, d), lambda i, j: (jnp.where(j < njl, i, b), 0, 0)),
                  full(wuv), full(wout), full(g), full(rw), full(rwt), full(rb)],
        out_specs=[blk(d), blk(d), blk(N_EXPERTS)],
        out_shape=[jax.ShapeDtypeStruct((b, nj * TQ, d), F32), jax.ShapeDtypeStruct((b, nj * TQ, d), BF16),
                   jax.ShapeDtypeStruct((b, nj * TQ, N_EXPERTS), F32)],
        compiler_params=_params(("arbitrary", "arbitrary")),
    )(x_all, na, df, ml, mod, wuv, wout, g, rw, rwt, rb)


def _moe_kernel(h_ref, gate_ref, x_ref, modl_ref, modc_ref, wgu_ref, wd_ref, sgu_ref, sd_ref, fg_ref,
                o_ref, acc_ref, act_ref, *, s_lat, tm, final):
    i2 = pl.program_id(1)
    e0 = pl.program_id(2)
    h = h_ref[0]
    ff = EXPERT_FF

    @pl.when(e0 == 0)
    def _():
        u = _dot(h, sgu_ref[...])
        acc_ref[...] = _dot((_silu(u[:, 0:ff]) * u[:, ff:2 * ff]).astype(BF16), sd_ref[...])

    gates = gate_ref[0]
    lane = lax.broadcasted_iota(jnp.int32, gates.shape, 1)
    for k in range(EXPERTS_PER_STEP):
        u = _dot(h, wgu_ref[k])
        gcol = jnp.sum(jnp.where(lane == e0 * EXPERTS_PER_STEP + k, gates, 0.0), axis=-1, keepdims=True)
        act_ref[:, k * ff:(k + 1) * ff] = (_silu(u[:, 0:ff]) * u[:, ff:2 * ff] * gcol).astype(BF16)
    acc_ref[...] += _dot(act_ref[...], wd_ref[0])

    @pl.when(e0 == pl.num_programs(2) - 1)
    def _():
        row = i2 * tm + lax.broadcasted_iota(jnp.int32, (tm, 1), 0)
        g2 = jnp.where(row < s_lat, modl_ref[0, 5:6, :], modc_ref[0, 5:6, :])
        y = x_ref[0] + g2 * acc_ref[...]
        if final:
            y = _rms(y) * fg_ref[...]
        o_ref[0] = y


def _moe(h2, gates, x1, mod, wgu, wd, sgu, sd, fg, s_lat, rows, final):
    b, t, d = x1.shape
    tm = rows // 2
    ne = wgu.shape[0] // EXPERTS_PER_STEP
    blk = lambda w: pl.BlockSpec((1, tm, w), lambda i, j, e: (i, j, 0))
    full = lambda a: pl.BlockSpec(a.shape, lambda i, j, e: (0,) * a.ndim)
    return pl.pallas_call(
        functools.partial(_moe_kernel, s_lat=s_lat, tm=tm, final=final),
        grid=(b, 2, ne),
        in_specs=[blk(d), blk(N_EXPERTS), blk(d),
                  pl.BlockSpec((1, 6, d), lambda i, j, e: (i, 0, 0)),
                  pl.BlockSpec((1, 6, d), lambda i, j, e: (b, 0, 0)),
                  pl.BlockSpec((EXPERTS_PER_STEP, d, 2 * EXPERT_FF), lambda i, j, e: (e, 0, 0)),
                  pl.BlockSpec((1, EXPERTS_PER_STEP * EXPERT_FF, d), lambda i, j, e: (e, 0, 0)),
                  full(sgu), full(sd), full(fg)],
        out_specs=blk(d),
        out_shape=jax.ShapeDtypeStruct((b, rows, d), F32),
        scratch_shapes=[pltpu.VMEM((tm, d), F32), pltpu.VMEM((tm, EXPERTS_PER_STEP * EXPERT_FF), BF16)],
        compiler_params=_params(("arbitrary", "arbitrary", "arbitrary")),
    )(h2, gates, x1, mod, mod, wgu, wd.reshape(ne, EXPERTS_PER_STEP * EXPERT_FF, d), sgu, sd, fg)


def _w_in_columns():
    z = IN_COLS
    cols = list(range(0, 3 * NA_W))
    for base in (O_DIFF, O_DIFF + DIFF_W):
        for half in range(2):
            for m in range(2 * DIFF_HEADS):
                cols += [base + m * DIFF_QK_DIM + half * HALF + i for i in range(HALF)]
    cols += list(range(O_DIFF + 2 * DIFF_W, O_MLA + MLA_Q_RANK + MLA_KV_RANK))
    kr = O_MLA + MLA_Q_RANK + MLA_KV_RANK
    cols += list(range(kr, kr + MLA_ROPE)) + [z] * (LANES - MLA_ROPE)
    rot = list(range(kr + HALF, kr + MLA_ROPE)) + list(range(kr, kr + HALF)) + [z] * (LANES - MLA_ROPE)
    sign = np.ones((W_IN_COLS,), np.float32)
    sign[len(cols):len(cols) + HALF] = -1.0
    cols += rot
    assert len(cols) == W_IN_COLS
    return np.asarray(cols, np.int32), sign


def _w_uq_columns():
    per = MLA_NOPE + MLA_ROPE
    cols = [h * per + i for h in range(MLA_HEADS) for i in range(MLA_NOPE)]
    for half in range(2):
        cols += [h * per + MLA_NOPE + half * HALF + i for h in range(MLA_HEADS) for i in range(HALF)]
    return np.asarray(cols, np.int32)


def _layer_weights(w_in, w_uq, w_ukv, w_out):
    cols, sign = _w_in_columns()
    win = (jnp.concatenate([w_in, jnp.zeros((w_in.shape[0], 1), F32)], axis=1)[:, cols] * sign).astype(BF16)
    wuq = w_uq[:, _w_uq_columns()].astype(BF16)
    kv = w_ukv.reshape(MLA_KV_RANK, MLA_HEADS, MLA_NOPE + MLA_V)
    wq2 = jnp.zeros((MLA_HEADS * MLA_NOPE + 2 * LANES, MLA_HEADS * SLOT), F32)
    place = np.zeros((2 * LANES, MLA_HEADS * SLOT), np.float32)
    for h in range(MLA_HEADS):
        wq2 = wq2.at[h * MLA_NOPE:(h + 1) * MLA_NOPE, h * SLOT:h * SLOT + MLA_KV_RANK].set(kv[:, h, :MLA_NOPE].T)
        for i in range(HALF):
            place[h * HALF + i, h * SLOT + MLA_KV_RANK + i] = 1.0
            place[LANES + h * HALF + i, h * SLOT + MLA_KV_RANK + HALF + i] = 1.0
    wq2 = wq2.at[MLA_HEADS * MLA_NOPE:, :].set(place).astype(BF16)
    wuv = jnp.zeros((MLA_HEADS // 2, SLOT, 2 * MLA_V), F32)
    for h in range(MLA_HEADS):
        wuv = wuv.at[h // 2, (h % 2) * LANES:(h % 2 + 1) * LANES, (h % 2) * MLA_V:(h % 2 + 1) * MLA_V].set(
            kv[:, h, MLA_NOPE:])
    return win, wuq, wq2, wuv.astype(BF16), w_out.astype(BF16)


def _rope_tables(s_lat, n_ctx):
    t = np.arange(s_lat)
    n_freq = ROPE_DIM // 4
    freq = ROPE_THETA ** (-jnp.arange(n_freq, dtype=F32) / n_freq)
    ang = jnp.concatenate([(t // GRID_W).astype(np.float32)[:, None] * freq,
                           (t % GRID_W).astype(np.float32)[:, None] * freq], axis=-1)
    cos, sin = jnp.cos(ang), jnp.sin(ang)
    one, zero = jnp.ones((n_ctx, LANES), F32), jnp.zeros((n_ctx, LANES), F32)
    rep = LANES // HALF
    cosd = jnp.concatenate([jnp.tile(cos, (1, rep)), one], axis=0)
    sind = jnp.concatenate([jnp.tile(sin, (1, rep)), zero], axis=0)
    return cosd, sind, cosd, sind


def _na_bias(rel_bias, s_lat):
    qc = np.arange(GRID_W)
    cs = np.clip(qc - WIN_C // 2, 0, GRID_W - WIN_C)
    col_ok = (qc[None, :] >= cs[:, None]) & (qc[None, :] < cs[:, None] + WIN_C)
    dc = qc[None, :] - qc[:, None] + WIN_C - 1
    pick = (dc[:, :, None] == np.arange(2 * WIN_C - 1)[None, None, :]) & col_ok[:, :, None]
    tiles = jnp.einsum('hij,cdj->hicd', rel_bias, jnp.asarray(pick, F32), precision=lax.Precision.HIGHEST)
    tiles = jnp.where(col_ok[None, None], tiles * LOG2E, NEG)
    masked = jnp.full((NA_HEADS, GRID_W, GRID_W), NEG, F32)
    out = []
    for delta, lo in ((0, 0), (-(WIN_R // 2), None), (-WIN_R, WIN_R // 2)):
        rows = []
        for qr in range(Q_ROWS):
            row = []
            for kr in range(NA_WIN_ROWS):
                rel = delta + kr - qr
                ok = (-(WIN_R // 2) <= rel < WIN_R // 2) if lo is None else (lo <= kr < lo + WIN_R)
                row.append(tiles[:, rel + WIN_R - 1] if ok else masked)
            rows.append(jnp.concatenate(row, axis=-1))
        out.append(jnp.concatenate(rows, axis=-2))
    return jnp.stack(out, axis=0)


def kernel(x, c, ctx, c_ctx, w_ada, b_ada, norm_attn_g, norm_ffn_g, w_in, na_rel_bias, diff_lambda, diff_subln_g, mla_q_norm_g, mla_w_uq, mla_kv_norm_g, mla_w_ukv, w_out, router_w, router_bias, expert_w_gate, expert_w_up, expert_w_down, shared_w_gate, shared_w_up, shared_w_down, final_norm_g):
    b, s_lat, d = x.shape
    n_ctx = ctx.shape[1]
    assert d == D_MODEL and n_ctx == TQ and s_lat % TQ == 0 and s_lat // GRID_W >= NA_WIN_ROWS + Q_ROWS
    t = s_lat + n_ctx

    c_rows = 32
    c_all = jnp.concatenate([c, c_ctx[None, :], jnp.zeros((c_rows - b - 1, d), F32)], axis=0)
    mod = _ada_mod(c_all, w_ada, b_ada)[:, :b + 1].reshape(DEPTH, b + 1, 6, d)

    tabs = _rope_tables(s_lat, n_ctx)
    row = lambda v: v.reshape(1, -1)
    xa = jnp.concatenate([x, ctx], axis=1)
    out = None
    for l in range(DEPTH):
        need_ctx = l < DEPTH - 1
        nj = t // TQ if need_ctx else s_lat // TQ
        lambda_init = 0.8 - 0.6 * math.exp(-0.3 * l)
        win, wuq, wq2, wuv, wout = _layer_weights(w_in[l], mla_w_uq[l], mla_w_ukv[l], w_out[l])
        qn, kn, vn, qd, kd, vd, qm, kvm = _pre(
            xa, mod[l], row(norm_attn_g[l]), win, row(mla_q_norm_g[l]), wuq, row(mla_kv_norm_g[l]), wq2, tabs, s_lat)
        gsub = row(jnp.tile(diff_subln_g[l], DIFF_HEADS) * (1.0 - lambda_init))
        na, df, ml = _attention(qn, kn, vn, _na_bias(na_rel_bias[l], s_lat), qd, kd, vd, diff_lambda[l], gsub,
                                qm, kvm, s_lat, n_ctx, nj, lambda_init)
        rb = jnp.broadcast_to(router_bias[l][:, None], (N_EXPERTS, TQ))
        x1, h2, gates = _post(xa, na, df, ml, mod[l], wuv, wout, row(norm_ffn_g[l]), router_w[l], router_w[l].T, rb,
                              s_lat, nj)
        wgu = jnp.concatenate([expert_w_gate[l], expert_w_up[l]], axis=-1).astype(BF16)
        sgu = jnp.concatenate([shared_w_gate[l], shared_w_up[l]], axis=-1).astype(BF16)
        res = _moe(h2, gates, x1, mod[l], wgu, expert_w_down[l].astype(BF16), sgu, shared_w_down[l].astype(BF16),
                   row(final_norm_g), s_lat, t if need_ctx else s_lat, final=not need_ctx)
        if need_ctx:
            xa = res
        else:
            out = res
    return out
```
